```python
import jax
import jax.numpy as jnp
from jax import lax
import numpy as np


D_MODEL = 1024
BATCH = 16
SEQ = 2048
DEPTH = 2

HEAD_DIM = 64
ROPE_THETA = 10000.0
Q_BLOCK = 128
LN_EPS = 1e-5
ALPHA = (2 * DEPTH) ** 0.25
BETA = (8 * DEPTH) ** -0.25

DIL_PATTERNS = ((128, 1), (512, 4), (2048, 16))
A_GROUPS = len(DIL_PATTERNS)
A_HEADS_PER_GROUP = 4
A_HEADS = A_GROUPS * A_HEADS_PER_GROUP
A_QKV = 3 * A_HEADS * HEAD_DIM
A_OUT = A_HEADS_PER_GROUP * HEAD_DIM

B_Q_HEADS = 8
B_KV_HEADS = 2
B_GROUP = B_Q_HEADS // B_KV_HEADS
B_Q = B_Q_HEADS * HEAD_DIM
B_KV = 3 * 2 * B_KV_HEADS * HEAD_DIM
B_GATE = 3 * B_Q_HEADS
CMP_STRIDE = 16
CMP_LEN = 2 * CMP_STRIDE
CMP_HIDDEN = 256
SLC_BLOCK = 64
SLC_TOPK = 8
WIN = 512

D_IN = A_QKV + B_Q + B_KV + B_GATE + 2 * D_MODEL

MEM_LEN = 256
X_HEADS = 4
X_DIM = X_HEADS * HEAD_DIM

D_FF = 2816
CONV_W = 3

kernel_name = 'hybrid_dilated_nsa_deepnorm_decoder'


def layer_norm(x, g, b):
    xf = x.astype(jnp.float32)
    mu = jnp.mean(xf, -1, keepdims=True)
    var = jnp.mean(jnp.square(xf - mu), -1, keepdims=True)
    y = (xf - mu) * lax.rsqrt(var + LN_EPS) * g.astype(jnp.float32) + b.astype(jnp.float32)
    return y.astype(x.dtype)


def rope(x, pos):
    half = HEAD_DIM // 2
    inv = ROPE_THETA ** (-jnp.arange(half, dtype=jnp.float32) / half)
    ang = pos.astype(jnp.float32)[:, None] * inv[None, :]
    cos = jnp.cos(ang)[None, :, None, :]
    sin = jnp.sin(ang)[None, :, None, :]
    xf = x.astype(jnp.float32)
    x1, x2 = xf[..., :half], xf[..., half:]
    return jnp.concatenate([x1 * cos - x2 * sin, x1 * sin + x2 * cos], -1).astype(x.dtype)


def masked_softmax(s, mask):
    s = jnp.where(mask, s.astype(jnp.float32), -jnp.inf)
    m = jnp.max(s, -1, keepdims=True)
    m = jnp.where(jnp.isfinite(m), m, 0.0)
    e = jnp.exp(s - m)
    l = jnp.maximum(jnp.sum(e, -1, keepdims=True), 1e-30)
    return e / l, (jnp.log(l) + m)[..., 0]


def dilated_window_attention(q, k, v, window, dilation):
    B, S, H, D = q.shape
    steps = window // dilation
    nq = Q_BLOCK // dilation
    nk = (window + Q_BLOCK) // dilation
    kp = jnp.pad(k, ((0, 0), (window, 0), (0, 0), (0, 0)))
    vp = jnp.pad(v, ((0, 0), (window, 0), (0, 0), (0, 0)))
    c = jnp.arange(nq)[:, None]
    a = jnp.arange(nk)[None, :]
    band = (c <= a) & (a <= steps + c)
    qb = q.reshape(B, S // Q_BLOCK, Q_BLOCK, H, D).swapaxes(0, 1)

    def block(args):
        i, qi = args
        s0 = i * Q_BLOCK
        kb = lax.dynamic_slice_in_dim(kp, s0, window + Q_BLOCK, axis=1).reshape(B, nk, dilation, H, D)
        vb = lax.dynamic_slice_in_dim(vp, s0, window + Q_BLOCK, axis=1).reshape(B, nk, dilation, H, D)
        qr = qi.reshape(B, nq, dilation, H, D)
        valid = band & (s0 - window + a * dilation >= 0)
        s = jnp.einsum('bcrhd,barhd->bhrca', qr, kb) * D ** -0.5
        p, lse = masked_softmax(s, valid)
        o = jnp.einsum('bhrca,barhd->bcrhd', p.astype(v.dtype), vb)
        return o.reshape(B, Q_BLOCK, H, D), lse.transpose(0, 3, 2, 1).reshape(B, Q_BLOCK, H)

    o, lse = lax.map(block, (jnp.arange(S // Q_BLOCK), qb))
    return o.swapaxes(0, 1).reshape(B, S, H, D), lse.swapaxes(0, 1).reshape(B, S, H)


def compress_blocks(x, pos_emb, w1, b1, w2):
    B, S, H, D = x.shape
    ch = x.reshape(B, S // CMP_STRIDE, CMP_STRIDE, H, D)
    blk = jnp.concatenate([ch[:, :-1], ch[:, 1:]], axis=2) + pos_emb[None, None, :, None, :]
    nc = blk.shape[1]
    flat = blk.transpose(0, 1, 3, 2, 4).reshape(B, nc, H, CMP_LEN * D)
    return jax.nn.gelu(flat @ w1 + b1) @ w2


def nsa_attention(q, kc, vc, ks, vs, kw, vw):
    B, S, Hq, D = q.shape
    Hkv = B_KV_HEADS
    nc = kc.shape[1]
    ns = S // SLC_BLOCK
    n_sel = min(SLC_TOPK, ns)
    scale = D ** -0.5
    c_start = CMP_STRIDE * jnp.arange(nc)
    c_end = c_start + CMP_LEN - 1
    s_start = SLC_BLOCK * jnp.arange(ns)
    overlap = jnp.clip(jnp.minimum(c_start[:, None] + CMP_LEN, s_start[None, :] + SLC_BLOCK)
                       - jnp.maximum(c_start[:, None], s_start[None, :]), 0).astype(jnp.float32) / CMP_LEN
    ks_blk = ks.reshape(B, ns, SLC_BLOCK, Hkv, D).transpose(0, 3, 1, 2, 4)
    vs_blk = vs.reshape(B, ns, SLC_BLOCK, Hkv, D).transpose(0, 3, 1, 2, 4)
    kw_p = jnp.pad(kw, ((0, 0), (WIN, 0), (0, 0), (0, 0)))
    vw_p = jnp.pad(vw, ((0, 0), (WIN, 0), (0, 0), (0, 0)))
    b_ix = jnp.arange(B)[:, None, None]
    h_ix = jnp.arange(Hkv)[None, :, None]
    blk_ids = jnp.arange(ns)
    qb = q.reshape(B, S // Q_BLOCK, Q_BLOCK, Hkv, B_GROUP, D).swapaxes(0, 1)

    def block(args):
        i, qi = args
        s0 = i * Q_BLOCK
        t = s0 + jnp.arange(Q_BLOCK)
        s = jnp.einsum('bqkgd,bnkd->bkgqn', qi, kc) * scale
        p_c, _ = masked_softmax(s, c_end[None, :] <= t[:, None])
        o_c = jnp.einsum('bkgqn,bnkd->bqkgd', p_c.astype(vc.dtype), vc)
        imp = jnp.einsum('bkgqn,nm->bkqm', p_c, overlap)
        cur = t // SLC_BLOCK
        forced = (blk_ids[None, :] == 0) | (blk_ids[None, :] == cur[:, None]) | (blk_ids[None, :] == cur[:, None] - 1)
        future = blk_ids[None, :] * SLC_BLOCK > t[:, None]
        imp = jnp.where(future, -jnp.inf, jnp.where(forced, jnp.inf, imp))
        _, idx = lax.top_k(imp, n_sel)
        idx_f = idx.reshape(B, Hkv, Q_BLOCK * n_sel)
        kg = ks_blk[b_ix, h_ix, idx_f].reshape(B, Hkv, Q_BLOCK, n_sel * SLC_BLOCK, D)
        vg = vs_blk[b_ix, h_ix, idx_f].reshape(B, Hkv, Q_BLOCK, n_sel * SLC_BLOCK, D)
        tok = (idx[..., None] * SLC_BLOCK + jnp.arange(SLC_BLOCK)).reshape(B, Hkv, Q_BLOCK, n_sel * SLC_BLOCK)
        s = jnp.einsum('bqkgd,bkqtd->bkgqt', qi, kg) * scale
        p_s, _ = masked_softmax(s, (tok <= t[None, None, :, None])[:, :, None])
        o_s = jnp.einsum('bkgqt,bkqtd->bqkgd', p_s.astype(vg.dtype), vg)
        kwb = lax.dynamic_slice_in_dim(kw_p, s0, WIN + Q_BLOCK, axis=1)
        vwb = lax.dynamic_slice_in_dim(vw_p, s0, WIN + Q_BLOCK, axis=1)
        kpos = s0 - WIN + jnp.arange(WIN + Q_BLOCK)
        dist = t[:, None] - kpos[None, :]
        wmask = (dist >= 0) & (dist < WIN) & (kpos[None, :] >= 0)
        s = jnp.einsum('bqkgd,btkd->bkgqt', qi, kwb) * scale
        p_w, _ = masked_softmax(s, wmask)
        o_w = jnp.einsum('bkgqt,btkd->bqkgd', p_w.astype(vw.dtype), vwb)
        return jnp.stack([o_c, o_s, o_w], axis=-2)

    o = lax.map(block, (jnp.arange(S // Q_BLOCK), qb))
    return o.swapaxes(0, 1).reshape(B, S, Hq, 3, D)


def hybrid_mixer(h, pos, w_in, cmp_pos, cmp_w1, cmp_b1, cmp_w2, w_branch_a, w_branch_b, w_mix_out):
    B, S, _ = h.shape
    z = h @ w_in
    o1 = A_QKV
    o2 = o1 + B_Q
    o3 = o2 + B_KV
    o4 = o3 + B_GATE
    o5 = o4 + D_MODEL
    z_a, z_bq, z_bkv, z_bg, z_ga, z_gb = jnp.split(z, [o1, o2, o3, o4, o5], axis=-1)

    qkv_a = z_a.reshape(B, S, 3, A_GROUPS, A_HEADS_PER_GROUP, HEAD_DIM)
    outs, lses = [], []
    for g, (window, dilation) in enumerate(DIL_PATTERNS):
        o_g, l_g = dilated_window_attention(rope(qkv_a[:, :, 0, g], pos), rope(qkv_a[:, :, 1, g], pos),
                                            qkv_a[:, :, 2, g], window, dilation)
        outs.append(o_g)
        lses.append(l_g)
    wts = jax.nn.softmax(jnp.stack(lses), axis=0)
    y_a = jnp.sum(wts[..., None].astype(h.dtype) * jnp.stack(outs), axis=0).reshape(B, S, A_OUT)

    q_b = rope(z_bq.reshape(B, S, B_Q_HEADS, HEAD_DIM), pos)
    kv = z_bkv.reshape(B, S, 3, 2, B_KV_HEADS, HEAD_DIM)
    kc = compress_blocks(kv[:, :, 0, 0], cmp_pos[0], cmp_w1[0], cmp_b1[0], cmp_w2[0])
    vc = compress_blocks(kv[:, :, 0, 1], cmp_pos[1], cmp_w1[1], cmp_b1[1], cmp_w2[1])
    kc = rope(kc, CMP_STRIDE * jnp.arange(kc.shape[1]) + CMP_LEN - 1)
    o_b = nsa_attention(q_b, kc, vc, rope(kv[:, :, 1, 0], pos), kv[:, :, 1, 1],
                        rope(kv[:, :, 2, 0], pos), kv[:, :, 2, 1])
    g_b = jax.nn.sigmoid(z_bg.reshape(B, S, B_Q_HEADS, 3))
    y_b = jnp.sum(g_b[..., None] * o_b, axis=-2).reshape(B, S, B_Q)

    merged = jax.nn.sigmoid(z_ga) * (y_a @ w_branch_a) + jax.nn.sigmoid(z_gb) * (y_b @ w_branch_b)
    return merged @ w_mix_out


def memory_cross_attention(h, mem, w_xq, w_xkv, w_xo):
    B, S, _ = h.shape
    M = mem.shape[1]
    q = (h @ w_xq).reshape(B, S, X_HEADS, HEAD_DIM)
    kv = (mem @ w_xkv).reshape(B, M, 2, X_HEADS, HEAD_DIM)
    s = jnp.einsum('bshd,bmhd->bhsm', q, kv[:, :, 0]).astype(jnp.float32) * HEAD_DIM ** -0.5
    p = jax.nn.softmax(s, axis=-1).astype(h.dtype)
    o = jnp.einsum('bhsm,bmhd->bshd', p, kv[:, :, 1]).reshape(B, S, X_DIM)
    return o @ w_xo


def causal_dwconv(u, w, b):
    C = u.shape[-1]
    y = lax.conv_general_dilated(u, w[:, None, :].astype(u.dtype), window_strides=(1,),
                                 padding=[(CONV_W - 1, 0)], dimension_numbers=('NWC', 'WIO', 'NWC'),
                                 feature_group_count=C)
    return y + b


def conv_ffn(h, w_up, conv_w, conv_b, w_down):
    u = causal_dwconv(h @ w_up, conv_w, conv_b)
    a, g = jnp.split(u, 2, axis=-1)
    return (jax.nn.gelu(a) * g) @ w_down


def _normal(key, shape, scale):
    return jax.random.normal(key, shape, jnp.float32) * scale


def setup_inputs(seed: int = 0) -> dict:
    key = jax.random.key(seed)
    ks = jax.random.split(key, 23)
    D = D_MODEL
    L = DEPTH
    return {
        'x': _normal(ks[0], (BATCH, SEQ, D), 1.0),
        'mem': _normal(ks[1], (BATCH, MEM_LEN, D), 1.0),
        'w_in': _normal(ks[2], (L, D, D_IN), D ** -0.5),
        'cmp_pos': _normal(ks[3], (L, 2, CMP_LEN, HEAD_DIM), 0.02),
        'cmp_w1': _normal(ks[4], (L, 2, CMP_LEN * HEAD_DIM, CMP_HIDDEN), (CMP_LEN * HEAD_DIM) ** -0.5),
        'cmp_b1': _normal(ks[5], (L, 2, CMP_HIDDEN), 0.01),
        'cmp_w2': _normal(ks[6], (L, 2, CMP_HIDDEN, HEAD_DIM), CMP_HIDDEN ** -0.5),
        'w_branch_a': _normal(ks[7], (L, A_OUT, D), A_OUT ** -0.5),
        'w_branch_b': _normal(ks[8], (L, B_Q, D), B_Q ** -0.5),
        'w_mix_out': _normal(ks[9], (L, D, D), BETA * D ** -0.5),
        'ln1_g': 1.0 + _normal(ks[10], (L, D), 0.02),
        'ln1_b': _normal(ks[11], (L, D), 0.02),
        'w_xq': _normal(ks[12], (L, D, X_DIM), D ** -0.5),
        'w_xkv': _normal(ks[13], (L, D, 2 * X_DIM), D ** -0.5),
        'w_xo': _normal(ks[14], (L, X_DIM, D), BETA * X_DIM ** -0.5),
        'ln2_g': 1.0 + _normal(ks[15], (L, D), 0.02),
        'ln2_b': _normal(ks[16], (L, D), 0.02),
        'w_up': _normal(ks[17], (L, D, 2 * D_FF), D ** -0.5),
        'conv_w': _normal(ks[18], (L, CONV_W, 2 * D_FF), CONV_W ** -0.5),
        'conv_b': _normal(ks[19], (L, 2 * D_FF), 0.01),
        'w_down': _normal(ks[20], (L, D_FF, D), BETA * D_FF ** -0.5),
        'ln3_g': 1.0 + _normal(ks[21], (L, D), 0.02),
        'ln3_b': _normal(ks[22], (L, D), 0.02),
    }


def reference(x, mem, w_in, cmp_pos, cmp_w1, cmp_b1, cmp_w2, w_branch_a, w_branch_b, w_mix_out,
              ln1_g, ln1_b, w_xq, w_xkv, w_xo, ln2_g, ln2_b, w_up, conv_w, conv_b, w_down, ln3_g, ln3_b):
    pos = jnp.arange(x.shape[1])
    h = x
    for l in range(DEPTH):
        mix = hybrid_mixer(h, pos, w_in[l], cmp_pos[l], cmp_w1[l], cmp_b1[l], cmp_w2[l],
                           w_branch_a[l], w_branch_b[l], w_mix_out[l])
        h = layer_norm(ALPHA * h + mix, ln1_g[l], ln1_b[l])
        h = layer_norm(ALPHA * h + memory_cross_attention(h, mem, w_xq[l], w_xkv[l], w_xo[l]), ln2_g[l], ln2_b[l])
        h = layer_norm(ALPHA * h + conv_ffn(h, w_up[l], conv_w[l], conv_b[l], w_down[l]), ln3_g[l], ln3_b[l])
    return h
```

```python
import functools

import jax
import jax.numpy as jnp
from jax import lax
from jax.experimental import pallas as pl
from jax.experimental.pallas import tpu as pltpu

F32 = jnp.float32
BF16 = jnp.bfloat16

D_MODEL = 1024
DEPTH = 2
HEAD_DIM = 64
HALF = HEAD_DIM // 2
ROPE_THETA = 10000.0
LN_EPS = 1e-5
ALPHA = (2 * DEPTH) ** 0.25
SCALE = HEAD_DIM ** -0.5

DIL_PATTERNS = ((128, 1), (512, 4), (2048, 16))
A_GROUPS = 3
A_HEADS_PER_GROUP = 4
A_QKV = 3 * A_GROUPS * A_HEADS_PER_GROUP * HEAD_DIM
A_OUT = A_HEADS_PER_GROUP * HEAD_DIM

B_Q_HEADS = 8
B_KV_HEADS = 2
B_GROUP = 4
B_Q = B_Q_HEADS * HEAD_DIM
B_KV = 3 * 2 * B_KV_HEADS * HEAD_DIM
B_GATE = 3 * B_Q_HEADS
CMP_STRIDE = 16
CMP_LEN = 32
CMP_HIDDEN = 256
SLC_BLOCK = 64
SLC_TOPK = 8
WIN = 512

X_HEADS = 4
X_DIM = X_HEADS * HEAD_DIM
D_FF = 2816
CONV_W = 3

LANES = 128
SUBLANES = 8
MXU_DIM = 256
VMEM_LIMIT_BYTES = 56 * 1024 * 1024

NEG = -1e30

N_PROJ_BLOCKS = 15
PROJ_COLS = N_PROJ_BLOCKS * MXU_DIM

_NT = (((1,), (1,)), ((), ()))


def _dot(a, b):
    return jnp.dot(a, b, preferred_element_type=F32)


def _dot_nt(a, b):
    return lax.dot_general(a, b, _NT, preferred_element_type=F32)


def _params(*sem):
    return pltpu.CompilerParams(dimension_semantics=sem, vmem_limit_bytes=VMEM_LIMIT_BYTES)


def _layer_norm(x, g, b):
    mu = jnp.mean(x, axis=-1, keepdims=True)
    xc = x - mu
    var = jnp.mean(xc * xc, axis=-1, keepdims=True)
    return xc * lax.rsqrt(var + LN_EPS) * g + b


def _rope128(z, c, s1, s2):
    return z * c + pltpu.roll(z, LANES - HALF, 1) * s1 + pltpu.roll(z, HALF, 1) * s2


def _proj_kernel(x_ref, w_ref, cq_ref, s1q_ref, s2q_ref, ck_ref, s1k_ref, s2k_ref,
                 za_ref, zq_ref, zkv_ref, zc_ref, zg_ref):
    xb = x_ref[...].astype(BF16)
    for j in range(N_PROJ_BLOCKS):
        z = _dot(xb, w_ref[:, j * MXU_DIM:(j + 1) * MXU_DIM])
        if j < 6 or 9 <= j <= 10:
            tabs = (cq_ref[...], s1q_ref[...], s2q_ref[...])
        elif 11 <= j <= 12:
            tabs = (ck_ref[...], s1k_ref[...], s2k_ref[...])
        else:
            tabs = None
        if tabs is not None:
            z = jnp.concatenate([_rope128(z[:, :LANES], *tabs), _rope128(z[:, LANES:], *tabs)], axis=1)
        if j < 9:
            za_ref[:, j * MXU_DIM:(j + 1) * MXU_DIM] = z
        elif j < 11:
            zq_ref[:, (j - 9) * MXU_DIM:(j - 8) * MXU_DIM] = z.astype(BF16)
        elif j < 13:
            zkv_ref[:, (j - 11) * MXU_DIM:(j - 10) * MXU_DIM] = z.astype(BF16)
        elif j == 13:
            zc_ref[...] = z
        else:
            zg_ref[...] = z


def _project(h2d, w, tabs, seq, tm):
    t = h2d.shape[0]
    nseq = seq // tm
    row = lambda i: (i, 0)
    tab_spec = pl.BlockSpec((tm, LANES), lambda i: (i % nseq, 0))
    return pl.pallas_call(
        _proj_kernel,
        grid=(t // tm,),
        in_specs=[pl.BlockSpec((tm, D_MODEL), row),
                  pl.BlockSpec((D_MODEL, PROJ_COLS), lambda i: (0, 0))] + [tab_spec] * 6,
        out_specs=[pl.BlockSpec((tm, A_QKV), row), pl.BlockSpec((tm, B_Q), row),
                   pl.BlockSpec((tm, 2 * MXU_DIM), row), pl.BlockSpec((tm, MXU_DIM), row),
                   pl.BlockSpec((tm, MXU_DIM), row)],
        out_shape=[jax.ShapeDtypeStruct((t, A_QKV), F32), jax.ShapeDtypeStruct((t, B_Q), BF16),
                   jax.ShapeDtypeStruct((t, 2 * MXU_DIM), BF16), jax.ShapeDtypeStruct((t, MXU_DIM), F32),
                   jax.ShapeDtypeStruct((t, MXU_DIM), F32)],
        compiler_params=_params("parallel"),
        name="proj",
    )(h2d, w, *tabs)


def _cmp_kernel(zc_ref, w1_ref, pe_ref, b1_ref, w2_ref, cos_ref, sin_ref, o_ref, xk_ref, xv_ref):
    seq = zc_ref.shape[0]
    nblk = seq // CMP_STRIDE
    xs_refs = (xk_ref, xv_ref)
    for kv in range(2):
        xs_refs[kv][0:seq, :] = zc_ref[:, kv * LANES:(kv + 1) * LANES]
        xs_refs[kv][seq:seq + CMP_LEN, :] = jnp.zeros((CMP_LEN, LANES), F32)
    acc = [jnp.zeros((nblk, CMP_HIDDEN), F32) for _ in range(4)]
    for j in range(CMP_LEN):
        xj = [r[pl.ds(j, nblk, stride=CMP_STRIDE), :] for r in xs_refs]
        for c in range(4):
            kv, hh = c // 2, c % 2
            piece = xj[kv][:, hh * HEAD_DIM:(hh + 1) * HEAD_DIM] + pe_ref[kv, j:j + 1, :]
            acc[c] = acc[c] + _dot(piece.astype(BF16), w1_ref[kv, j])
    outs = []
    for c in range(4):
        kv = c // 2
        hid = jax.nn.gelu(acc[c] + b1_ref[kv:kv + 1, :])
        y = _dot(hid.astype(BF16), w2_ref[kv])
        if kv == 0:
            y1, y2 = y[:, :HALF], y[:, HALF:]
            cs, sn = cos_ref[...], sin_ref[...]
            y = jnp.concatenate([y1 * cs - y2 * sn, y1 * sn + y2 * cs], axis=1)
        outs.append(y)
    o_ref[0] = jnp.concatenate([outs[0], outs[2]], axis=1)
    o_ref[1] = jnp.concatenate([outs[1], outs[3]], axis=1)


def _compress(zc, w1, pe, b1, w2, cos_c, sin_c):
    b, seq, _ = zc.shape
    nblk = seq // CMP_STRIDE
    full = lambda *shape: pl.BlockSpec(shape, lambda i: (0,) * len(shape))
    return pl.pallas_call(
        _cmp_kernel,
        grid=(b,),
        in_specs=[pl.BlockSpec((None, seq, MXU_DIM), lambda i: (i, 0, 0)),
                  full(2, CMP_LEN, HEAD_DIM, CMP_HIDDEN), full(2, CMP_LEN, HEAD_DIM),
                  full(2, CMP_HIDDEN), full(2, CMP_HIDDEN, HEAD_DIM),
                  full(nblk, HALF), full(nblk, HALF)],
        out_specs=pl.BlockSpec((None, 2, nblk, LANES), lambda i: (i, 0, 0, 0)),
        out_shape=jax.ShapeDtypeStruct((b, 2, nblk, LANES), F32),
        scratch_shapes=[pltpu.VMEM((seq + CMP_LEN, LANES), F32), pltpu.VMEM((seq + CMP_LEN, LANES), F32)],
        compiler_params=_params("parallel"),
        name="compress",
    )(zc, w1, pe, b1, w2, cos_c, sin_c)


A_TILE = 128


def _attn_a_kernel(q0, k0, v0, q1, k1, v1, q2, k2, v2, y_ref, o_sc, l_sc):
    seq = y_ref.shape[0]
    qkv = ((q0, k0, v0), (q1, k1, v1), (q2, k2, v2))
    row = lax.broadcasted_iota(jnp.int32, (A_TILE, A_TILE), 0)
    col = lax.broadcasted_iota(jnp.int32, (A_TILE, A_TILE), 1)
    first_head = col < HEAD_DIM
    causal = col <= row
    anti = col >= row

    for g, (_, dil) in enumerate(DIL_PATTERNS):
        q_ref, k_ref, v_ref = qkv[g]
        ntile = seq // dil // A_TILE
        span = A_TILE * dil

        def rows_at(start, dil=dil):
            if dil == 1:
                return pl.ds(pl.multiple_of(start, A_TILE), A_TILE)
            return pl.ds(start, A_TILE, stride=dil)

        def body(n, carry, g=g, q_ref=q_ref, k_ref=k_ref, v_ref=v_ref, ntile=ntile, span=span,
                 rows_at=rows_at):
            r = n // ntile
            i = n % ntile
            start = r + i * span
            rows = rows_at(start)
            q = q_ref[rows, :] * SCALE
            kc = k_ref[rows, :].astype(BF16)
            vc = v_ref[rows, :].astype(BF16)
            if ntile > 1:
                prows = rows_at(jnp.maximum(start - span, r))
                kp = k_ref[prows, :].astype(BF16)
                vp = v_ref[prows, :].astype(BF16)
                prev_ok = jnp.logical_and(anti, i > 0)
            o_heads, lse_heads = [], []
            for h2 in range(2):
                in_head = first_head if h2 == 0 else jnp.logical_not(first_head)
                qm = jnp.where(in_head, q, 0.0).astype(BF16)
                s_c = jnp.where(causal, _dot_nt(qm, kc), NEG)
                m = jnp.max(s_c, axis=-1, keepdims=True)
                if ntile > 1:
                    s_p = jnp.where(prev_ok, _dot_nt(qm, kp), NEG)
                    m = jnp.maximum(m, jnp.max(s_p, axis=-1, keepdims=True))
                e_c = jnp.exp(s_c - m)
                l = jnp.sum(e_c, axis=-1, keepdims=True)
                o = _dot(e_c.astype(BF16), vc)
                if ntile > 1:
                    e_p = jnp.exp(s_p - m)
                    l = l + jnp.sum(e_p, axis=-1, keepdims=True)
                    o = o + _dot(e_p.astype(BF16), vp)
                o_heads.append(o / l)
                lse_heads.append(jnp.log(l) + m)
            o_sc[g, rows, :] = jnp.where(first_head, o_heads[0], o_heads[1])
            l_sc[g, rows, :] = jnp.where(first_head, lse_heads[0], lse_heads[1])
            return carry

        lax.fori_loop(0, seq // A_TILE, body, 0)

    chunk = 256
    for c in range(seq // chunk):
        rows = pl.ds(c * chunk, chunk)
        ls = [l_sc[g, rows, :] for g in range(A_GROUPS)]
        mx = jnp.maximum(jnp.maximum(ls[0], ls[1]), ls[2])
        ws = [jnp.exp(l - mx) for l in ls]
        num = ws[0] * o_sc[0, rows, :] + ws[1] * o_sc[1, rows, :] + ws[2] * o_sc[2, rows, :]
        y_ref[rows, :] = (num / (ws[0] + ws[1] + ws[2])).astype(y_ref.dtype)


def _attn_a(za):
    b, seq, _ = za.shape
    in_specs = []
    for g in range(A_GROUPS):
        for part in range(3):
            blk = part * 2 * A_GROUPS + 2 * g
            in_specs.append(pl.BlockSpec((None, seq, LANES), lambda i, hp, blk=blk: (i, 0, blk + hp)))
    return pl.pallas_call(
        _attn_a_kernel,
        grid=(b, 2),
        in_specs=in_specs,
        out_specs=pl.BlockSpec((None, seq, LANES), lambda i, hp: (i, 0, hp)),
        out_shape=jax.ShapeDtypeStruct((b, seq, A_OUT), BF16),
        scratch_shapes=[pltpu.VMEM((A_GROUPS, seq, LANES), F32), pltpu.VMEM((A_GROUPS, seq, LANES), F32)],
        compiler_params=_params("parallel", "parallel"),
        name="attn_a",
    )(*([za] * 9))


NSA_TQ = 256
NSA_TK = 256
N_SEL_BLOCKS_MAX = 32


def _online_step(s, v, m, l, acc):
    m_new = jnp.maximum(m, jnp.max(s, axis=-1, keepdims=True))
    a = jnp.exp(m - m_new)
    p = jnp.exp(s - m_new)
    l = a * l + jnp.sum(p, axis=-1, keepdims=True)
    acc = a * acc + _dot(p.astype(BF16), v)
    return m_new, l, acc


def _nsa_kernel(q_ref, kv_ref, kvc_ref, zg_ref, ovt_ref, o_ref, kaug_ref):
    seq = kv_ref.shape[0]
    nsel = seq // SLC_BLOCK
    ncmp = kvc_ref.shape[0]
    tq = NSA_TQ
    mrows = B_GROUP * tq
    qi = pl.program_id(2)
    t0 = qi * tq

    @pl.when(qi == 0)
    def _():
        key = lax.broadcasted_iota(jnp.int32, (seq, LANES), 0)
        lane = lax.broadcasted_iota(jnp.int32, (seq, LANES), 1)
        onehot = jnp.logical_and(lane >= HEAD_DIM, (key // SLC_BLOCK) == lane - HEAD_DIM)
        kaug_ref[...] = jnp.where(lane < HEAD_DIM, kv_ref[:, :LANES],
                                  jnp.where(onehot, 1.0, 0.0).astype(BF16))

    lane_q = lax.broadcasted_iota(jnp.int32, (tq, LANES), 1)
    low = lane_q < HEAD_DIM
    qf = q_ref[...].astype(F32) * SCALE
    parts = []
    for pair in range(2):
        x = qf[:, pair * LANES:(pair + 1) * LANES]
        parts.append(jnp.where(low, x, 0.0))
        parts.append(jnp.where(low, pltpu.roll(x, HEAD_DIM, 1), 0.0))
    q128 = jnp.concatenate(parts, axis=0)
    qw = q128.astype(BF16)

    trow = t0 + (lax.broadcasted_iota(jnp.int32, (mrows, 1), 0) % tq)

    kvc = kvc_ref[...].astype(BF16)
    s = _dot_nt(qw, kvc)
    cend = lax.broadcasted_iota(jnp.int32, (1, ncmp), 1) * CMP_STRIDE + (CMP_LEN - 1)
    s = jnp.where(cend <= trow, s, -jnp.inf)
    m = jnp.max(s, axis=-1, keepdims=True)
    m = jnp.where(m == -jnp.inf, 0.0, m)
    e = jnp.exp(s - m)
    l_c = jnp.maximum(jnp.sum(e, axis=-1, keepdims=True), 1e-30)
    p = e / l_c
    acc_c = _dot(p.astype(BF16), kvc)

    psum = p[0:tq] + p[tq:2 * tq] + p[2 * tq:3 * tq] + p[3 * tq:4 * tq]
    p_hi = psum.astype(BF16)
    r1 = psum - p_hi.astype(F32)
    p_mid = r1.astype(BF16)
    p_lo = (r1 - p_mid.astype(F32)).astype(BF16)
    ovt = ovt_ref[...]
    imp = _dot_nt(ovt, p_hi) + _dot_nt(ovt, p_mid) + _dot_nt(ovt, p_lo)
    blk = lax.broadcasted_iota(jnp.int32, (nsel, tq), 0)
    tcol = t0 + lax.broadcasted_iota(jnp.int32, (nsel, tq), 1)
    cur = tcol // SLC_BLOCK
    forced = jnp.logical_or(blk == 0, jnp.logical_or(blk == cur, blk == cur - 1))
    future = blk * SLC_BLOCK > tcol
    imp = jnp.where(future, -jnp.inf, jnp.where(forced, jnp.inf, imp))
    rank = jnp.zeros((nsel, tq), F32)
    for mp in range(nsel):
        other = imp[mp:mp + 1, :]
        before = jnp.logical_or(other > imp, jnp.logical_and(other == imp, blk > mp))
        rank = rank + jnp.where(before, 1.0, 0.0)
    bias_t = jnp.where(rank < float(min(SLC_TOPK, nsel)), 0.0, NEG)
    bias_t = jnp.concatenate([jnp.zeros((HEAD_DIM, tq), F32), bias_t,
                              jnp.zeros((LANES - HEAD_DIM - nsel, tq), F32)], axis=0)
    bias = bias_t.T
    qs = (q128 + jnp.concatenate([bias] * B_GROUP, axis=0)).astype(BF16)

    rowi = lax.broadcasted_iota(jnp.int32, (mrows, NSA_TK), 0) % tq
    coli = lax.broadcasted_iota(jnp.int32, (mrows, NSA_TK), 1)
    causal = coli <= rowi

    def init():
        return (jnp.full((mrows, 1), NEG, F32), jnp.zeros((mrows, 1), F32), jnp.zeros((mrows, LANES), F32))

    def slc_body(j, carry):
        rows = pl.ds(pl.multiple_of(j * NSA_TK, NSA_TK), NSA_TK)
        return _online_step(_dot_nt(qs, kaug_ref[rows, :]), kv_ref[rows, :LANES], *carry)

    carry = lax.fori_loop(0, qi, slc_body, init())
    rows = pl.ds(pl.multiple_of(t0, NSA_TK), NSA_TK)
    s = jnp.where(causal, _dot_nt(qs, kaug_ref[rows, :]), NEG)
    _, l_s, acc_s = _online_step(s, kv_ref[rows, :LANES], *carry)

    carry = init()
    j2 = jnp.maximum(qi - 2, 0)
    rows2 = pl.ds(pl.multiple_of(j2 * NSA_TK, NSA_TK), NSA_TK)
    edge = jnp.logical_and(rowi - coli + 2 * NSA_TK < WIN, qi >= 2)
    s = jnp.where(edge, _dot_nt(qw, kv_ref[rows2, LANES:]), NEG)
    carry = _online_step(s, kv_ref[rows2, LANES:], *carry)
    j1 = jnp.maximum(qi - 1, 0)
    rows1 = pl.ds(pl.multiple_of(j1 * NSA_TK, NSA_TK), NSA_TK)
    s = jnp.where(qi >= 1, _dot_nt(qw, kv_ref[rows1, LANES:]), NEG)
    carry = _online_step(s, kv_ref[rows1, LANES:], *carry)
    s = jnp.where(causal, _dot_nt(qw, kv_ref[rows, LANES:]), NEG)
    _, l_w, acc_w = _online_step(s, kv_ref[rows, LANES:], *carry)

    gate = jax.nn.sigmoid(zg_ref[...])

    def gcol(br):
        return jnp.concatenate([gate[:, 3 * g + br:3 * g + br + 1] for g in range(B_GROUP)], axis=0)

    y = acc_c * gcol(0) + acc_s * (gcol(1) / l_s) + acc_w * (gcol(2) / l_w)
    for pair in range(2):
        even = y[(2 * pair) * tq:(2 * pair + 1) * tq]
        odd = y[(2 * pair + 1) * tq:(2 * pair + 2) * tq]
        o_ref[:, pair * LANES:(pair + 1) * LANES] = jnp.where(
            low, pltpu.roll(even, HEAD_DIM, 1), odd).astype(o_ref.dtype)


def _nsa(zq, zkv, kvc, zg, ovt):
    b, seq, _ = zq.shape
    ncmp = kvc.shape[2]
    nsel = seq // SLC_BLOCK
    return pl.pallas_call(
        _nsa_kernel,
        grid=(b, B_KV_HEADS, seq // NSA_TQ),
        in_specs=[pl.BlockSpec((None, NSA_TQ, MXU_DIM), lambda i, h, q: (i, q, h)),
                  pl.BlockSpec((None, seq, MXU_DIM), lambda i, h, q: (i, 0, h)),
                  pl.BlockSpec((None, None, ncmp, LANES), lambda i, h, q: (i, h, 0, 0)),
                  pl.BlockSpec((None, NSA_TQ, LANES), lambda i, h, q: (i, q, h)),
                  pl.BlockSpec((nsel, ncmp), lambda i, h, q: (0, 0))],
        out_specs=pl.BlockSpec((None, NSA_TQ, MXU_DIM), lambda i, h, q: (i, q, h)),
        out_shape=jax.ShapeDtypeStruct((b, seq, B_Q), BF16),
        scratch_shapes=[pltpu.VMEM((seq, LANES), BF16)],
        compiler_params=_params("parallel", "parallel", "arbitrary"),
        name="nsa",
    )(zq, zkv, kvc, zg, ovt)


def _merge_kernel(h_ref, ya_ref, yb_ref, wg_ref, wba_ref, wbb_ref, wmix_ref, g_ref, b_ref, o_ref):
    h = h_ref[...]
    hb = h.astype(BF16)
    ga = jax.nn.sigmoid(_dot(hb, wg_ref[:, :D_MODEL]))
    merged = ga * _dot(ya_ref[...], wba_ref[...])
    gb = jax.nn.sigmoid(_dot(hb, wg_ref[:, D_MODEL:]))
    merged = merged + gb * _dot(yb_ref[...], wbb_ref[...])
    mix = _dot(merged.astype(BF16), wmix_ref[...])
    o_ref[...] = _layer_norm(ALPHA * h + mix, g_ref[...], b_ref[...])


def _merge(h2d, ya, yb, wg, wba, wbb, wmix, g, b, tm):
    t = h2d.shape[0]
    row = lambda i: (i, 0)
    const = lambda i: (0, 0)
    return pl.pallas_call(
        _merge_kernel,
        grid=(t // tm,),
        in_specs=[pl.BlockSpec((tm, D_MODEL), row), pl.BlockSpec((tm, A_OUT), row),
                  pl.BlockSpec((tm, B_Q), row), pl.BlockSpec((D_MODEL, 2 * D_MODEL), const),
                  pl.BlockSpec((A_OUT, D_MODEL), const), pl.BlockSpec((B_Q, D_MODEL), const),
                  pl.BlockSpec((D_MODEL, D_MODEL), const), pl.BlockSpec((1, D_MODEL), const),
                  pl.BlockSpec((1, D_MODEL), const)],
        out_specs=pl.BlockSpec((tm, D_MODEL), row),
        out_shape=jax.ShapeDtypeStruct((t, D_MODEL), F32),
        compiler_params=_params("parallel"),
        name="merge",
    )(h2d, ya, yb, wg, wba, wbb, wmix, g, b)


def _xkv_kernel(mem_ref, w_ref, o_ref):
    o_ref[...] = _dot(mem_ref[...].astype(BF16), w_ref[...]).astype(o_ref.dtype)


def _xkv(mem, w):
    b, mlen, _ = mem.shape
    return pl.pallas_call(
        _xkv_kernel,
        grid=(b,),
        in_specs=[pl.BlockSpec((None, mlen, D_MODEL), lambda i: (i, 0, 0)),
                  pl.BlockSpec((D_MODEL, 2 * X_DIM), lambda i: (0, 0))],
        out_specs=pl.BlockSpec((None, mlen, 2 * X_DIM), lambda i: (i, 0, 0)),
        out_shape=jax.ShapeDtypeStruct((b, mlen, 2 * X_DIM), BF16),
        compiler_params=_params("parallel"),
        name="xkv",
    )(mem, w)


def _xattn_kernel(h_ref, kv_ref, wq_ref, wo_ref, g_ref, b_ref, o_ref):
    h = h_ref[...]
    tm = h.shape[0]
    q = _dot(h.astype(BF16), wq_ref[...]) * SCALE
    k = kv_ref[:, :X_DIM]
    v = kv_ref[:, X_DIM:]
    lane = lax.broadcasted_iota(jnp.int32, (tm, X_DIM), 1)
    o = jnp.zeros((tm, X_DIM), F32)
    for hd in range(X_HEADS):
        in_head = (lane // HEAD_DIM) == hd
        s = _dot_nt(jnp.where(in_head, q, 0.0).astype(BF16), k)
        m = jnp.max(s, axis=-1, keepdims=True)
        e = jnp.exp(s - m)
        p = e / jnp.sum(e, axis=-1, keepdims=True)
        o = jnp.where(in_head, _dot(p.astype(BF16), v), o)
    att = _dot(o.astype(BF16), wo_ref[...])
    o_ref[...] = _layer_norm(ALPHA * h + att, g_ref[...], b_ref[...])


def _xattn(h2d, kv, wq, wo, g, b, seq, tm):
    t = h2d.shape[0]
    mlen = kv.shape[1]
    nseq = seq // tm
    row = lambda i: (i, 0)
    const = lambda i: (0, 0)
    return pl.pallas_call(
        _xattn_kernel,
        grid=(t // tm,),
        in_specs=[pl.BlockSpec((tm, D_MODEL), row),
                  pl.BlockSpec((None, mlen, 2 * X_DIM), lambda i: (i // nseq, 0, 0)),
                  pl.BlockSpec((D_MODEL, X_DIM), const), pl.BlockSpec((X_DIM, D_MODEL), const),
                  pl.BlockSpec((1, D_MODEL), const), pl.BlockSpec((1, D_MODEL), const)],
        out_specs=pl.BlockSpec((tm, D_MODEL), row),
        out_shape=jax.ShapeDtypeStruct((t, D_MODEL), F32),
        compiler_params=_params("parallel"),
        name="xattn",
    )(h2d, kv, wq, wo, g, b)


FFN_CHUNK = 256
CARRY_ROWS = SUBLANES


def _ffn_kernel(h_ref, wup_ref, cw_ref, cb_ref, wdn_ref, g_ref, b_ref, o_ref, u_sc, *, tiles_per_seq):
    tm = h_ref.shape[0]
    i = pl.program_id(0)

    @pl.when(i % tiles_per_seq == 0)
    def _():
        u_sc[0:CARRY_ROWS, :] = jnp.zeros((CARRY_ROWS, 2 * D_FF), F32)

    h = h_ref[...]
    hb = h.astype(BF16)
    acc = jnp.zeros((tm, D_MODEL), F32)
    for c in range(D_FF // FFN_CHUNK):
        halves = []
        for base in (c * FFN_CHUNK, D_FF + c * FFN_CHUNK):
            cols = slice(base, base + FFN_CHUNK)
            u_sc[CARRY_ROWS:CARRY_ROWS + tm, cols] = _dot(hb, wup_ref[:, cols])
            y = (cw_ref[2:3, cols] * u_sc[CARRY_ROWS:CARRY_ROWS + tm, cols]
                 + cw_ref[1:2, cols] * u_sc[CARRY_ROWS - 1:CARRY_ROWS - 1 + tm, cols]
                 + cw_ref[0:1, cols] * u_sc[CARRY_ROWS - 2:CARRY_ROWS - 2 + tm, cols]
                 + cb_ref[:, cols])
            u_sc[CARRY_ROWS - 2:CARRY_ROWS, cols] = u_sc[CARRY_ROWS + tm - 2:CARRY_ROWS + tm, cols]
            halves.append(y)
        act = jax.nn.gelu(halves[0]) * halves[1]
        acc = acc + _dot(act.astype(BF16), wdn_ref[c * FFN_CHUNK:(c + 1) * FFN_CHUNK, :])
    o_ref[...] = _layer_norm(ALPHA * h + acc, g_ref[...], b_ref[...])


def _ffn(h2d, wup, cw, cb, wdn, g, b, seq, tm):
    t = h2d.shape[0]
    row = lambda i: (i, 0)
    const = lambda i: (0, 0)
    return pl.pallas_call(
        functools.partial(_ffn_kernel, tiles_per_seq=seq // tm),
        grid=(t // tm,),
        in_specs=[pl.BlockSpec((tm, D_MODEL), row), pl.BlockSpec((D_MODEL, 2 * D_FF), const),
                  pl.BlockSpec((CONV_W, 2 * D_FF), const), pl.BlockSpec((1, 2 * D_FF), const),
                  pl.BlockSpec((D_FF, D_MODEL), const), pl.BlockSpec((1, D_MODEL), const),
                  pl.BlockSpec((1, D_MODEL), const)],
        out_specs=pl.BlockSpec((tm, D_MODEL), row),
        out_shape=jax.ShapeDtypeStruct((t, D_MODEL), F32),
        scratch_shapes=[pltpu.VMEM((CARRY_ROWS + tm, 2 * D_FF), F32)],
        compiler_params=_params("arbitrary"),
        name="ffn",
    )(h2d, wup, cw, cb, wdn, g, b)


def _rope_tables(seq):
    inv = ROPE_THETA ** (-jnp.arange(HALF, dtype=F32) / HALF)
    ang = jnp.arange(seq, dtype=F32)[:, None] * inv[None, :]
    cos, sin = jnp.cos(ang), jnp.sin(ang)
    zero, one = jnp.zeros_like(cos), jnp.ones_like(cos)
    cq = jnp.concatenate([cos, cos, cos, cos], axis=1)
    s1q = jnp.concatenate([-sin, zero, -sin, zero], axis=1)
    s2q = jnp.concatenate([zero, sin, zero, sin], axis=1)
    ck = jnp.concatenate([cos, cos, one, one], axis=1)
    s1k = jnp.concatenate([-sin, zero, zero, zero], axis=1)
    s2k = jnp.concatenate([zero, sin, zero, zero], axis=1)
    pos_c = (CMP_STRIDE * jnp.arange(seq // CMP_STRIDE) + CMP_LEN - 1).astype(F32)
    ang_c = pos_c[:, None] * inv[None, :]
    return (cq, s1q, s2q, ck, s1k, s2k), jnp.cos(ang_c), jnp.sin(ang_c)


def _overlap_t(seq):
    ncmp = seq // CMP_STRIDE
    nsel = seq // SLC_BLOCK
    c_start = CMP_STRIDE * jnp.arange(ncmp)
    s_start = SLC_BLOCK * jnp.arange(nsel)
    ov = jnp.clip(jnp.minimum(c_start[None, :] + CMP_LEN, s_start[:, None] + SLC_BLOCK)
                  - jnp.maximum(c_start[None, :], s_start[:, None]), 0).astype(F32) / CMP_LEN
    return ov.astype(BF16)


def _pack_w_in(w):
    o1 = A_QKV
    o2 = o1 + B_Q
    o3 = o2 + B_KV
    o4 = o3 + B_GATE
    wa, wq, wkv, wbg, wgate = w[:, :o1], w[:, o1:o2], w[:, o2:o3], w[:, o3:o4], w[:, o4:]
    kv = wkv.reshape(D_MODEL, 3, 2, B_KV_HEADS, HEAD_DIM)
    per_head = [jnp.concatenate([kv[:, 1, 0, hh], kv[:, 1, 1, hh], kv[:, 2, 0, hh], kv[:, 2, 1, hh]], axis=1)
                for hh in range(B_KV_HEADS)]
    cmp_blk = wkv[:, :2 * B_KV_HEADS * HEAD_DIM]
    per_gate = B_GROUP * 3
    gates = [jnp.pad(wbg[:, hh * per_gate:(hh + 1) * per_gate], ((0, 0), (0, LANES - per_gate)))
             for hh in range(B_KV_HEADS)]
    wp = jnp.concatenate([wa, wq] + per_head + [cmp_blk] + gates, axis=1)
    return wp.astype(BF16), wgate.astype(BF16)


def kernel(x, mem, w_in, cmp_pos, cmp_w1, cmp_b1, cmp_w2, w_branch_a, w_branch_b, w_mix_out, ln1_g, ln1_b,
           w_xq, w_xkv, w_xo, ln2_g, ln2_b, w_up, conv_w, conv_b, w_down, ln3_g, ln3_b):
    b, seq, _ = x.shape
    t = b * seq
    tabs, cos_c, sin_c = _rope_tables(seq)
    ovt = _overlap_t(seq)
    h = x.reshape(t, D_MODEL)
    for l in range(DEPTH):
        wp, wgate = _pack_w_in(w_in[l])
        za, zq, zkv, zc, zg = _project(h, wp, tabs, seq, tm=512)
        kvc = _compress(zc.reshape(b, seq, MXU_DIM),
                        cmp_w1[l].reshape(2, CMP_LEN, HEAD_DIM, CMP_HIDDEN).astype(BF16),
                        cmp_pos[l], cmp_b1[l], cmp_w2[l].astype(BF16), cos_c, sin_c)
        ya = _attn_a(za.reshape(b, seq, A_QKV))
        yb = _nsa(zq.reshape(b, seq, B_Q), zkv.reshape(b, seq, 2 * MXU_DIM), kvc,
                  zg.reshape(b, seq, MXU_DIM), ovt)
        h = _merge(h, ya.reshape(t, A_OUT), yb.reshape(t, B_Q), wgate, w_branch_a[l].astype(BF16),
                   w_branch_b[l].astype(BF16), w_mix_out[l].astype(BF16),
                   ln1_g[l][None, :], ln1_b[l][None, :], tm=256)
        xkv = _xkv(mem, w_xkv[l].astype(BF16))
        h = _xattn(h, xkv, w_xq[l].astype(BF16), w_xo[l].astype(BF16),
                   ln2_g[l][None, :], ln2_b[l][None, :], seq, tm=256)
        h = _ffn(h, w_up[l].astype(BF16), conv_w[l], conv_b[l][None, :], w_down[l].astype(BF16),
                 ln3_g[l][None, :], ln3_b[l][None, :], seq, tm=256)
    return h.reshape(b, seq, D_MODEL)
```

```python
import functools

import jax
import jax.numpy as jnp
from jax import lax
from jax.experimental import pallas as pl
from jax.experimental.pallas import tpu as pltpu

F32 = jnp.float32
BF16 = jnp.bfloat16

D_MODEL = 1024
DEPTH = 2
HEAD_DIM = 64
HALF = HEAD_DIM // 2
ROPE_THETA = 10000.0
LN_EPS = 1e-5
ALPHA = (2 * DEPTH) ** 0.25
SCALE = HEAD_DIM ** -0.5

DIL_PATTERNS = ((128, 1), (512, 4), (2048, 16))
A_GROUPS = 3
A_HEADS_PER_GROUP = 4
A_QKV = 3 * A_GROUPS * A_HEADS_PER_GROUP * HEAD_DIM
A_OUT = A_HEADS_PER_GROUP * HEAD_DIM

B_Q_HEADS = 8
B_KV_HEADS = 2
B_GROUP = 4
B_Q = B_Q_HEADS * HEAD_DIM
B_KV = 3 * 2 * B_KV_HEADS * HEAD_DIM
B_GATE = 3 * B_Q_HEADS
CMP_STRIDE = 16
CMP_LEN = 32
CMP_HIDDEN = 256
SLC_BLOCK = 64
SLC_TOPK = 8
WIN = 512

X_HEADS = 4
X_DIM = X_HEADS * HEAD_DIM
D_FF = 2816
CONV_W = 3

LANES = 128
SUBLANES = 8
MXU_DIM = 256
VMEM_LIMIT_BYTES = 56 * 1024 * 1024

NEG = -1e30
LOG2E = 1.4426950408889634

N_PROJ_BLOCKS = 15
PROJ_COLS = N_PROJ_BLOCKS * MXU_DIM
PROJ_AHEAD = 2

_NT = (((1,), (1,)), ((), ()))


def _dot(a, b):
    return jnp.dot(a, b, preferred_element_type=F32)


def _dot_nt(a, b):
    return lax.dot_general(a, b, _NT, preferred_element_type=F32)


def _params(*sem):
    return pltpu.CompilerParams(dimension_semantics=sem, vmem_limit_bytes=VMEM_LIMIT_BYTES)


def _layer_norm(x, g, b):
    mu = jnp.mean(x, axis=-1, keepdims=True)
    xc = x - mu
    var = jnp.mean(xc * xc, axis=-1, keepdims=True)
    return xc * lax.rsqrt(var + LN_EPS) * g + b


def _rope128(z, c, s1, s2):
    return z * c + pltpu.roll(z, LANES - HALF, 1) * s1 + pltpu.roll(z, HALF, 1) * s2


def _proj_kernel(x_ref, w_ref, cq_ref, s1q_ref, s2q_ref, ck_ref, s1k_ref, s2k_ref,
                 za_ref, zq_ref, zkv_ref, zc_ref, zg_ref):
    xb = x_ref[...].astype(BF16)

    def block(j):
        return _dot(xb, w_ref[:, j * MXU_DIM:(j + 1) * MXU_DIM])

    ahead = [block(j) for j in range(PROJ_AHEAD)]
    for j in range(N_PROJ_BLOCKS):
        z = ahead.pop(0)
        if j + PROJ_AHEAD < N_PROJ_BLOCKS:
            ahead.append(block(j + PROJ_AHEAD))
        if j < 6 or 9 <= j <= 10:
            tabs = (cq_ref[...], s1q_ref[...], s2q_ref[...])
        elif 11 <= j <= 12:
            tabs = (ck_ref[...], s1k_ref[...], s2k_ref[...])
        else:
            tabs = None
        if tabs is not None:
            z = jnp.concatenate([_rope128(z[:, :LANES], *tabs), _rope128(z[:, LANES:], *tabs)], axis=1)
        if j < 9:
            za_ref[:, j * MXU_DIM:(j + 1) * MXU_DIM] = z
        elif j < 11:
            zq_ref[:, (j - 9) * MXU_DIM:(j - 8) * MXU_DIM] = z.astype(BF16)
        elif j < 13:
            zkv_ref[:, (j - 11) * MXU_DIM:(j - 10) * MXU_DIM] = z.astype(BF16)
        elif j == 13:
            zc_ref[...] = z
        else:
            zg_ref[...] = z


def _project(h2d, w, tabs, seq, tm):
    t = h2d.shape[0]
    nseq = seq // tm
    row = lambda i: (i, 0)
    tab_spec = pl.BlockSpec((tm, LANES), lambda i: (i % nseq, 0))
    return pl.pallas_call(
        _proj_kernel,
        grid=(t // tm,),
        in_specs=[pl.BlockSpec((tm, D_MODEL), row),
                  pl.BlockSpec((D_MODEL, PROJ_COLS), lambda i: (0, 0))] + [tab_spec] * 6,
        out_specs=[pl.BlockSpec((tm, A_QKV), row), pl.BlockSpec((tm, B_Q), row),
                   pl.BlockSpec((tm, 2 * MXU_DIM), row), pl.BlockSpec((tm, MXU_DIM), row),
                   pl.BlockSpec((tm, MXU_DIM), row)],
        out_shape=[jax.ShapeDtypeStruct((t, A_QKV), F32), jax.ShapeDtypeStruct((t, B_Q), BF16),
                   jax.ShapeDtypeStruct((t, 2 * MXU_DIM), BF16), jax.ShapeDtypeStruct((t, MXU_DIM), F32),
                   jax.ShapeDtypeStruct((t, MXU_DIM), F32)],
        compiler_params=_params("parallel"),
        name="proj",
    )(h2d, w, *tabs)


def _cmp_kernel(zc_ref, w1_ref, pe_ref, b1_ref, w2_ref, cos_ref, sin_ref, o_ref, xk_ref, xv_ref):
    seq = zc_ref.shape[0]
    nblk = seq // CMP_STRIDE
    xs_refs = (xk_ref, xv_ref)
    for kv in range(2):
        xs_refs[kv][0:seq, :] = zc_ref[:, kv * LANES:(kv + 1) * LANES]
        xs_refs[kv][seq:seq + CMP_LEN, :] = jnp.zeros((CMP_LEN, LANES), F32)
    acc = [jnp.zeros((nblk, CMP_HIDDEN), F32) for _ in range(4)]
    for j in range(CMP_LEN):
        xj = [r[pl.ds(j, nblk, stride=CMP_STRIDE), :] for r in xs_refs]
        for c in range(4):
            kv, hh = c // 2, c % 2
            piece = xj[kv][:, hh * HEAD_DIM:(hh + 1) * HEAD_DIM] + pe_ref[kv, j:j + 1, :]
            acc[c] = acc[c] + _dot(piece.astype(BF16), w1_ref[kv, j])
    outs = []
    for c in range(4):
        kv = c // 2
        hid = jax.nn.gelu(acc[c] + b1_ref[kv:kv + 1, :])
        y = _dot(hid.astype(BF16), w2_ref[kv])
        if kv == 0:
            y1, y2 = y[:, :HALF], y[:, HALF:]
            cs, sn = cos_ref[...], sin_ref[...]
            y = jnp.concatenate([y1 * cs - y2 * sn, y1 * sn + y2 * cs], axis=1)
        outs.append(y)
    o_ref[0] = jnp.concatenate([outs[0], outs[2]], axis=1)
    o_ref[1] = jnp.concatenate([outs[1], outs[3]], axis=1)


def _compress(zc, w1, pe, b1, w2, cos_c, sin_c):
    b, seq, _ = zc.shape
    nblk = seq // CMP_STRIDE
    full = lambda *shape: pl.BlockSpec(shape, lambda i: (0,) * len(shape))
    return pl.pallas_call(
        _cmp_kernel,
        grid=(b,),
        in_specs=[pl.BlockSpec((None, seq, MXU_DIM), lambda i: (i, 0, 0)),
                  full(2, CMP_LEN, HEAD_DIM, CMP_HIDDEN), full(2, CMP_LEN, HEAD_DIM),
                  full(2, CMP_HIDDEN), full(2, CMP_HIDDEN, HEAD_DIM),
                  full(nblk, HALF), full(nblk, HALF)],
        out_specs=pl.BlockSpec((None, 2, nblk, LANES), lambda i: (i, 0, 0, 0)),
        out_shape=jax.ShapeDtypeStruct((b, 2, nblk, LANES), F32),
        scratch_shapes=[pltpu.VMEM((seq + CMP_LEN, LANES), F32), pltpu.VMEM((seq + CMP_LEN, LANES), F32)],
        compiler_params=_params("parallel"),
        name="compress",
    )(zc, w1, pe, b1, w2, cos_c, sin_c)


A_TILE = 128
A_UNROLL = 4


def _attn_a_kernel(q0, k0, v0, q1, k1, v1, q2, k2, v2, y_ref, o_sc, l_sc):
    seq = y_ref.shape[0]
    qkv = ((q0, k0, v0), (q1, k1, v1), (q2, k2, v2))
    first_head = lax.broadcasted_iota(jnp.int32, (A_TILE, LANES), 1) < HEAD_DIM

    for g, (_, dil) in enumerate(DIL_PATTERNS):
        q_ref, k_ref, v_ref = qkv[g]
        ntile = seq // dil // A_TILE
        span = A_TILE * dil
        nkeys = 2 * A_TILE if ntile > 1 else A_TILE
        row = lax.broadcasted_iota(jnp.int32, (A_TILE, nkeys), 0)
        col = lax.broadcasted_iota(jnp.int32, (A_TILE, nkeys), 1)
        band = jnp.logical_and(col >= row, col <= row + A_TILE) if ntile > 1 else col <= row
        own = col >= A_TILE

        def rows_at(start, dil=dil):
            if dil == 1:
                return pl.ds(pl.multiple_of(start, A_TILE), A_TILE)
            return pl.ds(start, A_TILE, stride=dil)

        def tile(n, g=g, q_ref=q_ref, k_ref=k_ref, v_ref=v_ref, ntile=ntile, span=span,
                 rows_at=rows_at, band=band, own=own):
            r = n // ntile
            i = n % ntile
            start = r + i * span
            rows = rows_at(start)
            q = q_ref[rows, :] * (SCALE * LOG2E)
            k = k_ref[rows, :].astype(BF16)
            v = v_ref[rows, :].astype(BF16)
            valid = band
            if ntile > 1:
                prows = rows_at(jnp.maximum(start - span, r))
                k = jnp.concatenate([k_ref[prows, :].astype(BF16), k], axis=0)
                v = jnp.concatenate([v_ref[prows, :].astype(BF16), v], axis=0)
                valid = jnp.logical_and(band, jnp.logical_or(own, i > 0))
            o_heads, lse_heads = [], []
            for h2 in range(2):
                in_head = first_head if h2 == 0 else jnp.logical_not(first_head)
                qm = jnp.where(in_head, q, 0.0).astype(BF16)
                s = jnp.where(valid, _dot_nt(qm, k), NEG)
                m = jnp.max(s, axis=-1, keepdims=True)
                e = jnp.exp2(s - m)
                l = jnp.sum(e, axis=-1, keepdims=True)
                o_heads.append(_dot(e.astype(BF16), v) / l)
                lse_heads.append(jnp.log2(l) + m)
            o_sc[g, rows, :] = jnp.where(first_head, o_heads[0], o_heads[1])
            l_sc[g, rows, :] = jnp.where(first_head, lse_heads[0], lse_heads[1])

        def body(it, carry, tile=tile):
            for u in range(A_UNROLL):
                tile(it * A_UNROLL + u)
            return carry

        lax.fori_loop(0, seq // A_TILE // A_UNROLL, body, 0)

    chunk = 256
    for c in range(seq // chunk):
        rows = pl.ds(c * chunk, chunk)
        ls = [l_sc[g, rows, :] for g in range(A_GROUPS)]
        mx = jnp.maximum(jnp.maximum(ls[0], ls[1]), ls[2])
        ws = [jnp.exp2(l - mx) for l in ls]
        num = ws[0] * o_sc[0, rows, :] + ws[1] * o_sc[1, rows, :] + ws[2] * o_sc[2, rows, :]
        y_ref[rows, :] = (num / (ws[0] + ws[1] + ws[2])).astype(y_ref.dtype)


def _attn_a(za):
    b, seq, _ = za.shape
    in_specs = []
    for g in range(A_GROUPS):
        for part in range(3):
            blk = part * 2 * A_GROUPS + 2 * g
            in_specs.append(pl.BlockSpec((None, seq, LANES), lambda i, hp, blk=blk: (i, 0, blk + hp)))
    return pl.pallas_call(
        _attn_a_kernel,
        grid=(b, 2),
        in_specs=in_specs,
        out_specs=pl.BlockSpec((None, seq, LANES), lambda i, hp: (i, 0, hp)),
        out_shape=jax.ShapeDtypeStruct((b, seq, A_OUT), BF16),
        scratch_shapes=[pltpu.VMEM((A_GROUPS, seq, LANES), F32), pltpu.VMEM((A_GROUPS, seq, LANES), F32)],
        compiler_params=_params("parallel", "parallel"),
        name="attn_a",
    )(*([za] * 9))


NSA_TQ = 128
NSA_TK = 512
NSA_SPLIT = 1


def _scores(q, k):
    part = q.shape[0] // NSA_SPLIT
    return tuple(_dot_nt(q[c * part:(c + 1) * part], k) for c in range(NSA_SPLIT))


def _softmax_step(s, v, m, acc, mask=None):
    if mask is not None:
        mask = jnp.concatenate([mask] * (s[0].shape[0] // mask.shape[0]), axis=0)
    ms, accs = [], []
    for c in range(NSA_SPLIT):
        sc = s[c] if mask is None else jnp.where(mask, s[c], NEG)
        m_new = jnp.maximum(m[c], jnp.max(sc, axis=-1, keepdims=True))
        accs.append(jnp.exp2(m[c] - m_new) * acc[c] + _dot(jnp.exp2(sc - m_new).astype(BF16), v))
        ms.append(m_new)
    return tuple(ms), tuple(accs)


def _nsa_kernel(q_ref, kv_ref, kvc_ref, zg_ref, ovt_ref, o_ref, kaug_ref, vs_ref, vw_ref):
    seq = kv_ref.shape[0]
    nsel = seq // SLC_BLOCK
    ncmp = kvc_ref.shape[0]
    tq = NSA_TQ
    mrows = B_GROUP * tq
    qi = pl.program_id(2)
    t0 = qi * tq

    @pl.when(qi == 0)
    def _():
        key = lax.broadcasted_iota(jnp.int32, (seq, LANES), 0)
        lane = lax.broadcasted_iota(jnp.int32, (seq, LANES), 1)
        onehot = jnp.logical_and(lane >= HEAD_DIM, (key // SLC_BLOCK) == lane - HEAD_DIM)
        kaug_ref[...] = jnp.where(lane < HEAD_DIM, kv_ref[:, :LANES],
                                  jnp.where(onehot, 1.0, 0.0).astype(BF16))
        one = jnp.ones((seq, LANES), BF16)
        vs_ref[...] = jnp.where(lane < HEAD_DIM, one, kv_ref[:, :LANES])
        vw_ref[...] = jnp.where(lane < HEAD_DIM, one, kv_ref[:, LANES:])

    lane_q = lax.broadcasted_iota(jnp.int32, (tq, LANES), 1)
    low = lane_q < HEAD_DIM
    qf = q_ref[...].astype(F32) * (SCALE * LOG2E)
    parts = []
    for pair in range(2):
        x = qf[:, pair * LANES:(pair + 1) * LANES]
        parts.append(jnp.where(low, x, 0.0))
        parts.append(jnp.where(low, pltpu.roll(x, HEAD_DIM, 1), 0.0))
    q128 = jnp.concatenate(parts, axis=0)
    qw = q128.astype(BF16)

    trow = t0 + (lax.broadcasted_iota(jnp.int32, (mrows, 1), 0) % tq)
    init = ((jnp.full((mrows // NSA_SPLIT, 1), NEG, F32),) * NSA_SPLIT,
            (jnp.zeros((mrows // NSA_SPLIT, LANES), F32),) * NSA_SPLIT)

    kvc = kvc_ref[...].astype(BF16)
    s = _dot_nt(qw, kvc)
    cend = lax.broadcasted_iota(jnp.int32, (1, ncmp), 1) * CMP_STRIDE + (CMP_LEN - 1)
    s = jnp.where(cend <= trow, s, -jnp.inf)
    m = jnp.max(s, axis=-1, keepdims=True)
    m = jnp.where(m == -jnp.inf, 0.0, m)
    e = jnp.exp2(s - m)
    l_c = jnp.maximum(jnp.sum(e, axis=-1, keepdims=True), 1e-30)
    p = e / l_c
    acc_c = _dot(p.astype(BF16), kvc)

    psum = p[0:tq] + p[tq:2 * tq] + p[2 * tq:3 * tq] + p[3 * tq:4 * tq]
    p_hi = psum.astype(BF16)
    r1 = psum - p_hi.astype(F32)
    p_mid = r1.astype(BF16)
    p_lo = (r1 - p_mid.astype(F32)).astype(BF16)
    ovt = ovt_ref[...]
    imp = _dot_nt(ovt, p_hi) + _dot_nt(ovt, p_mid) + _dot_nt(ovt, p_lo)
    blk = lax.broadcasted_iota(jnp.int32, (nsel, tq), 0)
    tcol = t0 + lax.broadcasted_iota(jnp.int32, (nsel, tq), 1)
    cur = tcol // SLC_BLOCK
    forced = jnp.logical_or(blk == 0, jnp.logical_or(blk == cur, blk == cur - 1))
    future = blk * SLC_BLOCK > tcol
    imp = jnp.where(future, -jnp.inf, jnp.where(forced, jnp.inf, imp))
    rank = jnp.zeros((nsel, tq), F32)
    for mp in range(nsel):
        other = imp[mp:mp + 1, :]
        before = jnp.logical_or(other > imp, jnp.logical_and(other == imp, blk > mp))
        rank = rank + jnp.where(before, 1.0, 0.0)
    bias_t = jnp.where(rank < float(min(SLC_TOPK, nsel)), 0.0, NEG)
    bias_t = jnp.concatenate([jnp.zeros((HEAD_DIM, tq), F32), bias_t,
                              jnp.zeros((LANES - HEAD_DIM - nsel, tq), F32)], axis=0)
    bias = bias_t.T
    qs = (q128 + jnp.concatenate([bias] * B_GROUP, axis=0)).astype(BF16)

    w0 = jnp.maximum(t0 - WIN, 0)
    wrows = pl.ds(pl.multiple_of(w0, tq), WIN + tq)
    dist = (t0 - w0) + (lax.broadcasted_iota(jnp.int32, (tq, WIN + tq), 0)
                        - lax.broadcasted_iota(jnp.int32, (tq, WIN + tq), 1))
    in_window = jnp.logical_and(dist >= 0, dist < WIN)
    _, acc_w = _softmax_step(_scores(qw, kv_ref[wrows, LANES:]), vw_ref[wrows, :], *init, mask=in_window)

    def key_rows(j):
        return pl.ds(pl.multiple_of(j * NSA_TK, NSA_TK), NSA_TK)

    def slc_body(j, carry):
        s, m, acc = carry
        s_next = _scores(qs, kaug_ref[key_rows(j + 1), :])
        m, acc = _softmax_step(s, vs_ref[key_rows(j), :], m, acc)
        return s_next, m, acc

    last = t0 // NSA_TK
    s, m, acc = lax.fori_loop(0, last, slc_body, (_scores(qs, kaug_ref[key_rows(0), :]),) + init)
    dist = (lax.broadcasted_iota(jnp.int32, (tq, NSA_TK), 0)
            - lax.broadcasted_iota(jnp.int32, (tq, NSA_TK), 1))
    _, acc_s = _softmax_step(s, vs_ref[key_rows(last), :], m, acc, mask=dist >= last * NSA_TK - t0)

    gate = jax.nn.sigmoid(zg_ref[...])

    def gcol(br):
        return jnp.concatenate([gate[:, 3 * g + br:3 * g + br + 1] for g in range(B_GROUP)], axis=0)

    acc_s = jnp.concatenate(acc_s, axis=0)
    acc_w = jnp.concatenate(acc_w, axis=0)
    l_s = pltpu.roll(acc_s, HEAD_DIM, 1)
    l_w = pltpu.roll(acc_w, HEAD_DIM, 1)
    y = acc_c * gcol(0) + acc_s * (gcol(1) / l_s) + acc_w * (gcol(2) / l_w)
    for pair in range(2):
        even = y[(2 * pair) * tq:(2 * pair + 1) * tq]
        odd = y[(2 * pair + 1) * tq:(2 * pair + 2) * tq]
        o_ref[:, pair * LANES:(pair + 1) * LANES] = jnp.where(
            low, pltpu.roll(even, HEAD_DIM, 1), odd).astype(o_ref.dtype)


def _nsa(zq, zkv, kvc, zg, ovt):
    b, seq, _ = zq.shape
    ncmp = kvc.shape[2]
    nsel = seq // SLC_BLOCK
    return pl.pallas_call(
        _nsa_kernel,
        grid=(b, B_KV_HEADS, seq // NSA_TQ),
        in_specs=[pl.BlockSpec((None, NSA_TQ, MXU_DIM), lambda i, h, q: (i, q, h)),
                  pl.BlockSpec((None, seq, MXU_DIM), lambda i, h, q: (i, 0, h)),
                  pl.BlockSpec((None, None, ncmp, LANES), lambda i, h, q: (i, h, 0, 0)),
                  pl.BlockSpec((None, NSA_TQ, LANES), lambda i, h, q: (i, q, h)),
                  pl.BlockSpec((nsel, ncmp), lambda i, h, q: (0, 0))],
        out_specs=pl.BlockSpec((None, NSA_TQ, MXU_DIM), lambda i, h, q: (i, q, h)),
        out_shape=jax.ShapeDtypeStruct((b, seq, B_Q), BF16),
        scratch_shapes=[pltpu.VMEM((seq, LANES), BF16)] * 3,
        compiler_params=_params("parallel", "parallel", "arbitrary"),
        name="nsa",
    )(zq, zkv, kvc, zg, ovt)


def _merge_kernel(h_ref, ya_ref, yb_ref, wg_ref, wba_ref, wbb_ref, wmix_ref, g_ref, b_ref, o_ref):
    h = h_ref[...]
    hb = h.astype(BF16)
    ya = ya_ref[...]
    yb = yb_ref[...]
    nchunk = D_MODEL // MXU_DIM

    def pre(n):
        cols = slice(n * MXU_DIM, (n + 1) * MXU_DIM)
        gcols = slice(D_MODEL + n * MXU_DIM, D_MODEL + (n + 1) * MXU_DIM)
        return (_dot(hb, wg_ref[:, cols]), _dot(ya, wba_ref[:, cols]),
                _dot(hb, wg_ref[:, gcols]), _dot(yb, wbb_ref[:, cols]))

    ahead = [pre(n) for n in range(2)]
    mix = jnp.zeros(h.shape, F32)
    for n in range(nchunk):
        za, pa, zb, pb = ahead.pop(0)
        if n + 2 < nchunk:
            ahead.append(pre(n + 2))
        merged = jax.nn.sigmoid(za) * pa + jax.nn.sigmoid(zb) * pb
        mix = mix + _dot(merged.astype(BF16), wmix_ref[n * MXU_DIM:(n + 1) * MXU_DIM, :])
    o_ref[...] = _layer_norm(ALPHA * h + mix, g_ref[...], b_ref[...])


def _merge(h2d, ya, yb, wg, wba, wbb, wmix, g, b, tm):
    t = h2d.shape[0]
    row = lambda i: (i, 0)
    const = lambda i: (0, 0)
    return pl.pallas_call(
        _merge_kernel,
        grid=(t // tm,),
        in_specs=[pl.BlockSpec((tm, D_MODEL), row), pl.BlockSpec((tm, A_OUT), row),
                  pl.BlockSpec((tm, B_Q), row), pl.BlockSpec((D_MODEL, 2 * D_MODEL), const),
                  pl.BlockSpec((A_OUT, D_MODEL), const), pl.BlockSpec((B_Q, D_MODEL), const),
                  pl.BlockSpec((D_MODEL, D_MODEL), const), pl.BlockSpec((1, D_MODEL), const),
                  pl.BlockSpec((1, D_MODEL), const)],
        out_specs=pl.BlockSpec((tm, D_MODEL), row),
        out_shape=jax.ShapeDtypeStruct((t, D_MODEL), F32),
        compiler_params=_params("parallel"),
        name="merge",
    )(h2d, ya, yb, wg, wba, wbb, wmix, g, b)


def _xkv_kernel(mem_ref, w_ref, o_ref):
    o_ref[...] = _dot(mem_ref[...].astype(BF16), w_ref[...]).astype(o_ref.dtype)


def _xkv(mem, w):
    b, mlen, _ = mem.shape
    return pl.pallas_call(
        _xkv_kernel,
        grid=(b,),
        in_specs=[pl.BlockSpec((None, mlen, D_MODEL), lambda i: (i, 0, 0)),
                  pl.BlockSpec((D_MODEL, 2 * X_DIM), lambda i: (0, 0))],
        out_specs=pl.BlockSpec((None, mlen, 2 * X_DIM), lambda i: (i, 0, 0)),
        out_shape=jax.ShapeDtypeStruct((b, mlen, 2 * X_DIM), BF16),
        compiler_params=_params("parallel"),
        name="xkv",
    )(mem, w)


def _xattn_kernel(h_ref, kv_ref, wq_ref, wo_ref, g_ref, b_ref, o_ref):
    h = h_ref[...]
    tm = h.shape[0]
    q = _dot(h.astype(BF16), wq_ref[...]) * SCALE
    k = kv_ref[:, :X_DIM]
    v = kv_ref[:, X_DIM:]
    lane = lax.broadcasted_iota(jnp.int32, (tm, X_DIM), 1)
    o = jnp.zeros((tm, X_DIM), F32)
    for hd in range(X_HEADS):
        in_head = (lane // HEAD_DIM) == hd
        s = _dot_nt(jnp.where(in_head, q, 0.0).astype(BF16), k)
        m = jnp.max(s, axis=-1, keepdims=True)
        e = jnp.exp(s - m)
        p = e / jnp.sum(e, axis=-1, keepdims=True)
        o = jnp.where(in_head, _dot(p.astype(BF16), v), o)
    att = _dot(o.astype(BF16), wo_ref[...])
    o_ref[...] = _layer_norm(ALPHA * h + att, g_ref[...], b_ref[...])


def _xattn(h2d, kv, wq, wo, g, b, seq, tm):
    t = h2d.shape[0]
    mlen = kv.shape[1]
    nseq = seq // tm
    row = lambda i: (i, 0)
    const = lambda i: (0, 0)
    return pl.pallas_call(
        _xattn_kernel,
        grid=(t // tm,),
        in_specs=[pl.BlockSpec((tm, D_MODEL), row),
                  pl.BlockSpec((None, mlen, 2 * X_DIM), lambda i: (i // nseq, 0, 0)),
                  pl.BlockSpec((D_MODEL, X_DIM), const), pl.BlockSpec((X_DIM, D_MODEL), const),
                  pl.BlockSpec((1, D_MODEL), const), pl.BlockSpec((1, D_MODEL), const)],
        out_specs=pl.BlockSpec((tm, D_MODEL), row),
        out_shape=jax.ShapeDtypeStruct((t, D_MODEL), F32),
        compiler_params=_params("parallel"),
        name="xattn",
    )(h2d, kv, wq, wo, g, b)


FFN_CHUNK = 256
FFN_AHEAD = 2
CARRY_ROWS = SUBLANES


def _ffn_kernel(h_ref, wup_ref, cw_ref, cb_ref, wdn_ref, g_ref, b_ref, o_ref, u_sc, *, tiles_per_seq):
    tm = h_ref.shape[0]
    i = pl.program_id(0)

    @pl.when(i % tiles_per_seq == 0)
    def _():
        u_sc[0:CARRY_ROWS, :] = jnp.zeros((CARRY_ROWS, 2 * D_FF), F32)

    h = h_ref[...]
    hb = h.astype(BF16)
    acc = jnp.zeros((tm, D_MODEL), F32)
    nchunk = D_FF // FFN_CHUNK

    def up(c):
        return [_dot(hb, wup_ref[:, base:base + FFN_CHUNK]) for base in (c * FFN_CHUNK, D_FF + c * FFN_CHUNK)]

    ahead = [up(c) for c in range(FFN_AHEAD)]
    for c in range(nchunk):
        u_cur = ahead.pop(0)
        if c + FFN_AHEAD < nchunk:
            ahead.append(up(c + FFN_AHEAD))
        halves = []
        for u, base in zip(u_cur, (c * FFN_CHUNK, D_FF + c * FFN_CHUNK)):
            cols = slice(base, base + FFN_CHUNK)
            u_sc[CARRY_ROWS:CARRY_ROWS + tm, cols] = u
            y = (cw_ref[2:3, cols] * u
                 + cw_ref[1:2, cols] * u_sc[CARRY_ROWS - 1:CARRY_ROWS - 1 + tm, cols]
                 + cw_ref[0:1, cols] * u_sc[CARRY_ROWS - 2:CARRY_ROWS - 2 + tm, cols]
                 + cb_ref[:, cols])
            u_sc[CARRY_ROWS - 2:CARRY_ROWS, cols] = u_sc[CARRY_ROWS + tm - 2:CARRY_ROWS + tm, cols]
            halves.append(y)
        act = jax.nn.gelu(halves[0]) * halves[1]
        acc = acc + _dot(act.astype(BF16), wdn_ref[c * FFN_CHUNK:(c + 1) * FFN_CHUNK, :])
    o_ref[...] = _layer_norm(ALPHA * h + acc, g_ref[...], b_ref[...])


def _ffn(h2d, wup, cw, cb, wdn, g, b, seq, tm):
    t = h2d.shape[0]
    row = lambda i: (i, 0)
    const = lambda i: (0, 0)
    return pl.pallas_call(
        functools.partial(_ffn_kernel, tiles_per_seq=seq // tm),
        grid=(t // tm,),
        in_specs=[pl.BlockSpec((tm, D_MODEL), row), pl.BlockSpec((D_MODEL, 2 * D_FF), const),
                  pl.BlockSpec((CONV_W, 2 * D_FF), const), pl.BlockSpec((1, 2 * D_FF), const),
                  pl.BlockSpec((D_FF, D_MODEL), const), pl.BlockSpec((1, D_MODEL), const),
                  pl.BlockSpec((1, D_MODEL), const)],
        out_specs=pl.BlockSpec((tm, D_MODEL), row),
        out_shape=jax.ShapeDtypeStruct((t, D_MODEL), F32),
        scratch_shapes=[pltpu.VMEM((CARRY_ROWS + tm, 2 * D_FF), F32)],
        compiler_params=_params("arbitrary"),
        name="ffn",
    )(h2d, wup, cw, cb, wdn, g, b)


def _rope_tables(seq):
    inv = ROPE_THETA ** (-jnp.arange(HALF, dtype=F32) / HALF)
    ang = jnp.arange(seq, dtype=F32)[:, None] * inv[None, :]
    cos, sin = jnp.cos(ang), jnp.sin(ang)
    zero, one = jnp.zeros_like(cos), jnp.ones_like(cos)
    cq = jnp.concatenate([cos, cos, cos, cos], axis=1)
    s1q = jnp.concatenate([-sin, zero, -sin, zero], axis=1)
    s2q = jnp.concatenate([zero, sin, zero, sin], axis=1)
    ck = jnp.concatenate([cos, cos, one, one], axis=1)
    s1k = jnp.concatenate([-sin, zero, zero, zero], axis=1)
    s2k = jnp.concatenate([zero, sin, zero, zero], axis=1)
    pos_c = (CMP_STRIDE * jnp.arange(seq // CMP_STRIDE) + CMP_LEN - 1).astype(F32)
    ang_c = pos_c[:, None] * inv[None, :]
    return (cq, s1q, s2q, ck, s1k, s2k), jnp.cos(ang_c), jnp.sin(ang_c)


def _overlap_t(seq):
    ncmp = seq // CMP_STRIDE
    nsel = seq // SLC_BLOCK
    c_start = CMP_STRIDE * jnp.arange(ncmp)
    s_start = SLC_BLOCK * jnp.arange(nsel)
    ov = jnp.clip(jnp.minimum(c_start[None, :] + CMP_LEN, s_start[:, None] + SLC_BLOCK)
                  - jnp.maximum(c_start[None, :], s_start[:, None]), 0).astype(F32) / CMP_LEN
    return ov.astype(BF16)


def _pack_w_in(w):
    o1 = A_QKV
    o2 = o1 + B_Q
    o3 = o2 + B_KV
    o4 = o3 + B_GATE
    wa, wq, wkv, wbg, wgate = w[:, :o1], w[:, o1:o2], w[:, o2:o3], w[:, o3:o4], w[:, o4:]
    kv = wkv.reshape(D_MODEL, 3, 2, B_KV_HEADS, HEAD_DIM)
    per_head = [jnp.concatenate([kv[:, 1, 0, hh], kv[:, 1, 1, hh], kv[:, 2, 0, hh], kv[:, 2, 1, hh]], axis=1)
                for hh in range(B_KV_HEADS)]
    cmp_blk = wkv[:, :2 * B_KV_HEADS * HEAD_DIM]
    per_gate = B_GROUP * 3
    gates = [jnp.pad(wbg[:, hh * per_gate:(hh + 1) * per_gate], ((0, 0), (0, LANES - per_gate)))
             for hh in range(B_KV_HEADS)]
    wp = jnp.concatenate([wa, wq] + per_head + [cmp_blk] + gates, axis=1)
    return wp.astype(BF16), wgate.astype(BF16)


def kernel(x, mem, w_in, cmp_pos, cmp_w1, cmp_b1, cmp_w2, w_branch_a, w_branch_b, w_mix_out, ln1_g, ln1_b,
           w_xq, w_xkv, w_xo, ln2_g, ln2_b, w_up, conv_w, conv_b, w_down, ln3_g, ln3_b):
    b, seq, _ = x.shape
    t = b * seq
    tabs, cos_c, sin_c = _rope_tables(seq)
    ovt = _overlap_t(seq)
    h = x.reshape(t, D_MODEL)
    for l in range(DEPTH):
        wp, wgate = _pack_w_in(w_in[l])
        za, zq, zkv, zc, zg = _project(h, wp, tabs, seq, tm=512)
        kvc = _compress(zc.reshape(b, seq, MXU_DIM),
                        cmp_w1[l].reshape(2, CMP_LEN, HEAD_DIM, CMP_HIDDEN).astype(BF16),
                        cmp_pos[l], cmp_b1[l], cmp_w2[l].astype(BF16), cos_c, sin_c)
        ya = _attn_a(za.reshape(b, seq, A_QKV))
        yb = _nsa(zq.reshape(b, seq, B_Q), zkv.reshape(b, seq, 2 * MXU_DIM), kvc,
                  zg.reshape(b, seq, MXU_DIM), ovt)
        h = _merge(h, ya.reshape(t, A_OUT), yb.reshape(t, B_Q), wgate, w_branch_a[l].astype(BF16),
                   w_branch_b[l].astype(BF16), w_mix_out[l].astype(BF16),
                   ln1_g[l][None, :], ln1_b[l][None, :], tm=256)
        xkv = _xkv(mem, w_xkv[l].astype(BF16))
        h = _xattn(h, xkv, w_xq[l].astype(BF16), w_xo[l].astype(BF16),
                   ln2_g[l][None, :], ln2_b[l][None, :], seq, tm=256)
        h = _ffn(h, w_up[l].astype(BF16), conv_w[l], conv_b[l][None, :], w_down[l].astype(BF16),
                 ln3_g[l][None, :], ln3_b[l][None, :], seq, tm=256)
    return h.reshape(b, seq, D_MODEL)
```

```python
import functools

import jax
import jax.numpy as jnp
from jax import lax
from jax.experimental import pallas as pl
from jax.experimental.pallas import tpu as pltpu

F32 = jnp.float32
BF16 = jnp.bfloat16

D_MODEL = 1024
DEPTH = 2
HEAD_DIM = 64
HALF = HEAD_DIM // 2
ROPE_THETA = 10000.0
LN_EPS = 1e-5
ALPHA = (2 * DEPTH) ** 0.25
SCALE = HEAD_DIM ** -0.5

DIL_PATTERNS = ((128, 1), (512, 4), (2048, 16))
A_GROUPS = 3
A_HEADS_PER_GROUP = 4
A_QKV = 3 * A_GROUPS * A_HEADS_PER_GROUP * HEAD_DIM
A_OUT = A_HEADS_PER_GROUP * HEAD_DIM

B_Q_HEADS = 8
B_KV_HEADS = 2
B_GROUP = 4
B_Q = B_Q_HEADS * HEAD_DIM
B_KV = 3 * 2 * B_KV_HEADS * HEAD_DIM
B_GATE = 3 * B_Q_HEADS
CMP_STRIDE = 16
CMP_LEN = 32
CMP_HIDDEN = 256
SLC_BLOCK = 64
SLC_TOPK = 8
WIN = 512

X_HEADS = 4
X_DIM = X_HEADS * HEAD_DIM
D_FF = 2816
CONV_W = 3

LANES = 128
SUBLANES = 8
MXU_DIM = 256
VMEM_LIMIT_BYTES = 56 * 1024 * 1024

NEG = -1e30
LOG2E = 1.4426950408889634

N_PROJ_BLOCKS = 15
PROJ_COLS = N_PROJ_BLOCKS * MXU_DIM
PROJ_AHEAD = 2

_NT = (((1,), (1,)), ((), ()))


def _dot(a, b):
    return jnp.dot(a, b, preferred_element_type=F32)


def _dot_nt(a, b):
    return lax.dot_general(a, b, _NT, preferred_element_type=F32)


def _params(*sem):
    return pltpu.CompilerParams(dimension_semantics=sem, vmem_limit_bytes=VMEM_LIMIT_BYTES)


def _layer_norm(x, g, b):
    mu = jnp.mean(x, axis=-1, keepdims=True)
    xc = x - mu
    var = jnp.mean(xc * xc, axis=-1, keepdims=True)
    return xc * lax.rsqrt(var + LN_EPS) * g + b


def _rope128(z, c, s1, s2):
    return z * c + pltpu.roll(z, LANES - HALF, 1) * s1 + pltpu.roll(z, HALF, 1) * s2


def _proj_kernel(x_ref, w_ref, cq_ref, s1q_ref, s2q_ref, ck_ref, s1k_ref, s2k_ref,
                 za_ref, zq_ref, zkv_ref, zc_ref, zg_ref):
    xb = x_ref[...].astype(BF16)

    def block(j):
        return _dot(xb, w_ref[:, j * MXU_DIM:(j + 1) * MXU_DIM])

    ahead = [block(j) for j in range(PROJ_AHEAD)]
    for j in range(N_PROJ_BLOCKS):
        z = ahead.pop(0)
        if j + PROJ_AHEAD < N_PROJ_BLOCKS:
            ahead.append(block(j + PROJ_AHEAD))
        if j < 6 or 9 <= j <= 10:
            tabs = (cq_ref[...], s1q_ref[...], s2q_ref[...])
        elif 11 <= j <= 12:
            tabs = (ck_ref[...], s1k_ref[...], s2k_ref[...])
        else:
            tabs = None
        if tabs is not None:
            z = jnp.concatenate([_rope128(z[:, :LANES], *tabs), _rope128(z[:, LANES:], *tabs)], axis=1)
        if j < 9:
            za_ref[:, j * MXU_DIM:(j + 1) * MXU_DIM] = z
        elif j < 11:
            zq_ref[:, (j - 9) * MXU_DIM:(j - 8) * MXU_DIM] = z.astype(BF16)
        elif j < 13:
            zkv_ref[:, (j - 11) * MXU_DIM:(j - 10) * MXU_DIM] = z.astype(BF16)
        elif j == 13:
            zc_ref[...] = z
        else:
            zg_ref[...] = z


def _project(h2d, w, tabs, seq, tm):
    t = h2d.shape[0]
    nseq = seq // tm
    row = lambda i: (i, 0)
    tab_spec = pl.BlockSpec((tm, LANES), lambda i: (i % nseq, 0))
    return pl.pallas_call(
        _proj_kernel,
        grid=(t // tm,),
        in_specs=[pl.BlockSpec((tm, D_MODEL), row),
                  pl.BlockSpec((D_MODEL, PROJ_COLS), lambda i: (0, 0))] + [tab_spec] * 6,
        out_specs=[pl.BlockSpec((tm, A_QKV), row), pl.BlockSpec((tm, B_Q), row),
                   pl.BlockSpec((tm, 2 * MXU_DIM), row), pl.BlockSpec((tm, MXU_DIM), row),
                   pl.BlockSpec((tm, MXU_DIM), row)],
        out_shape=[jax.ShapeDtypeStruct((t, A_QKV), F32), jax.ShapeDtypeStruct((t, B_Q), BF16),
                   jax.ShapeDtypeStruct((t, 2 * MXU_DIM), BF16), jax.ShapeDtypeStruct((t, MXU_DIM), F32),
                   jax.ShapeDtypeStruct((t, MXU_DIM), F32)],
        compiler_params=_params("parallel"),
        name="proj",
    )(h2d, w, *tabs)


def _cmp_kernel(zc_ref, w1_ref, pe_ref, b1_ref, w2_ref, cos_ref, sin_ref, o_ref, xk_ref, xv_ref):
    seq = zc_ref.shape[0]
    nblk = seq // CMP_STRIDE
    xs_refs = (xk_ref, xv_ref)
    for kv in range(2):
        xs_refs[kv][0:seq, :] = zc_ref[:, kv * LANES:(kv + 1) * LANES]
        xs_refs[kv][seq:seq + CMP_LEN, :] = jnp.zeros((CMP_LEN, LANES), F32)
    acc = [jnp.zeros((nblk, CMP_HIDDEN), F32) for _ in range(4)]
    for j in range(CMP_LEN):
        xj = [r[pl.ds(j, nblk, stride=CMP_STRIDE), :] for r in xs_refs]
        for c in range(4):
            kv, hh = c // 2, c % 2
            piece = xj[kv][:, hh * HEAD_DIM:(hh + 1) * HEAD_DIM] + pe_ref[kv, j:j + 1, :]
            acc[c] = acc[c] + _dot(piece.astype(BF16), w1_ref[kv, j])
    outs = []
    for c in range(4):
        kv = c // 2
        hid = jax.nn.gelu(acc[c] + b1_ref[kv:kv + 1, :])
        y = _dot(hid.astype(BF16), w2_ref[kv])
        if kv == 0:
            y1, y2 = y[:, :HALF], y[:, HALF:]
            cs, sn = cos_ref[...], sin_ref[...]
            y = jnp.concatenate([y1 * cs - y2 * sn, y1 * sn + y2 * cs], axis=1)
        outs.append(y)
    o_ref[0] = jnp.concatenate([outs[0], outs[2]], axis=1)
    o_ref[1] = jnp.concatenate([outs[1], outs[3]], axis=1)


def _compress(zc, w1, pe, b1, w2, cos_c, sin_c):
    b, seq, _ = zc.shape
    nblk = seq // CMP_STRIDE
    full = lambda *shape: pl.BlockSpec(shape, lambda i: (0,) * len(shape))
    return pl.pallas_call(
        _cmp_kernel,
        grid=(b,),
        in_specs=[pl.BlockSpec((None, seq, MXU_DIM), lambda i: (i, 0, 0)),
                  full(2, CMP_LEN, HEAD_DIM, CMP_HIDDEN), full(2, CMP_LEN, HEAD_DIM),
                  full(2, CMP_HIDDEN), full(2, CMP_HIDDEN, HEAD_DIM),
                  full(nblk, HALF), full(nblk, HALF)],
        out_specs=pl.BlockSpec((None, 2, nblk, LANES), lambda i: (i, 0, 0, 0)),
        out_shape=jax.ShapeDtypeStruct((b, 2, nblk, LANES), F32),
        scratch_shapes=[pltpu.VMEM((seq + CMP_LEN, LANES), F32), pltpu.VMEM((seq + CMP_LEN, LANES), F32)],
        compiler_params=_params("parallel"),
        name="compress",
    )(zc, w1, pe, b1, w2, cos_c, sin_c)


A_TILE = 128
A_UNROLL = 8
FREE_STRIDE = 4


def _attn_a_kernel(q0, k0, v0, q1, k1, v1, q2, k2, v2, y_ref, o_sc, l_sc, tq_sc, tk_sc, tv_sc, to_sc, tl_sc):
    seq = y_ref.shape[0]
    qkv = ((q0, k0, v0), (q1, k1, v1), (q2, k2, v2))
    first_head = lax.broadcasted_iota(jnp.int32, (A_TILE, LANES), 1) < HEAD_DIM

    for g, (_, dil) in enumerate(DIL_PATTERNS):
        q_ref, k_ref, v_ref = qkv[g]
        o_dst, l_dst = o_sc.at[g], l_sc.at[g]
        ntile = seq // dil // A_TILE
        span = A_TILE * dil
        nkeys = 2 * A_TILE if ntile > 1 else A_TILE
        row = lax.broadcasted_iota(jnp.int32, (A_TILE, nkeys), 0)
        col = lax.broadcasted_iota(jnp.int32, (A_TILE, nkeys), 1)
        band = jnp.logical_and(col >= row, col <= row + A_TILE) if ntile > 1 else col <= row
        own = col >= A_TILE
        two_stage = dil > FREE_STRIDE
        if two_stage:
            assert ntile == 1 and dil % FREE_STRIDE == 0
            part = seq // FREE_STRIDE
            srcs = (q_ref, k_ref, v_ref)
            q_ref, k_ref, v_ref, o_dst, l_dst = tq_sc, tk_sc, tv_sc, to_sc, tl_sc
            for src, dst in zip(srcs, (q_ref, k_ref, v_ref)):
                for r4 in range(FREE_STRIDE):
                    dst[r4 * part:(r4 + 1) * part, :] = src[pl.ds(r4, part, stride=FREE_STRIDE), :]

        def rows_at(start, dil=dil, two_stage=two_stage):
            if dil == 1:
                return pl.ds(pl.multiple_of(start, A_TILE), A_TILE)
            if two_stage:
                return pl.ds((start % FREE_STRIDE) * (seq // FREE_STRIDE) + start // FREE_STRIDE, A_TILE,
                             stride=dil // FREE_STRIDE)
            return pl.ds(start, A_TILE, stride=dil)

        def load(n, q_ref=q_ref, k_ref=k_ref, v_ref=v_ref, ntile=ntile, span=span, rows_at=rows_at,
                 band=band, own=own):
            r = n // ntile
            i = n % ntile
            start = r + i * span
            rows = rows_at(start)
            q = q_ref[rows, :] * (SCALE * LOG2E)
            q2 = jnp.concatenate([jnp.where(first_head, q, 0.0), jnp.where(first_head, 0.0, q)], axis=0).astype(BF16)
            k = k_ref[rows, :].astype(BF16)
            v = v_ref[rows, :].astype(BF16)
            valid = band
            if ntile > 1:
                prows = rows_at(jnp.maximum(start - span, r))
                k = jnp.concatenate([k_ref[prows, :].astype(BF16), k], axis=0)
                v = jnp.concatenate([v_ref[prows, :].astype(BF16), v], axis=0)
                valid = jnp.logical_and(band, jnp.logical_or(own, i > 0))
            return rows, q2, k, v, jnp.concatenate([valid, valid], axis=0)

        def body(it, carry, load=load, o_dst=o_dst, l_dst=l_dst):
            tiles = [load(it * A_UNROLL + u) for u in range(A_UNROLL)]
            scores = [jnp.where(valid, _dot_nt(q2, k), NEG) for _, q2, k, _, valid in tiles]
            for (rows, _, _, v, _), s in zip(tiles, scores):
                m = jnp.max(s, axis=-1, keepdims=True)
                e = jnp.exp2(s - m)
                l = jnp.sum(e, axis=-1, keepdims=True)
                o = _dot(e.astype(BF16), v) / l
                lse = jnp.broadcast_to(jnp.log2(l) + m, o.shape)
                o_dst[rows, :] = jnp.where(first_head, o[:A_TILE], o[A_TILE:])
                l_dst[rows, :] = jnp.where(first_head, lse[:A_TILE], lse[A_TILE:])
            return carry

        lax.fori_loop(0, seq // A_TILE // A_UNROLL, body, 0)
        if two_stage:
            for src, dst in ((o_dst, o_sc.at[g]), (l_dst, l_sc.at[g])):
                for r4 in range(FREE_STRIDE):
                    dst[pl.ds(r4, part, stride=FREE_STRIDE), :] = src[r4 * part:(r4 + 1) * part, :]

    chunk = 256
    for c in range(seq // chunk):
        rows = pl.ds(c * chunk, chunk)
        ls = [l_sc[g, rows, :] for g in range(A_GROUPS)]
        mx = jnp.maximum(jnp.maximum(ls[0], ls[1]), ls[2])
        ws = [jnp.exp2(l - mx) for l in ls]
        num = ws[0] * o_sc[0, rows, :] + ws[1] * o_sc[1, rows, :] + ws[2] * o_sc[2, rows, :]
        y_ref[rows, :] = (num / (ws[0] + ws[1] + ws[2])).astype(y_ref.dtype)


def _attn_a(za):
    b, seq, _ = za.shape
    in_specs = []
    for g in range(A_GROUPS):
        for part in range(3):
            blk = part * 2 * A_GROUPS + 2 * g
            in_specs.append(pl.BlockSpec((None, seq, LANES), lambda i, hp, blk=blk: (i, 0, blk + hp)))
    return pl.pallas_call(
        _attn_a_kernel,
        grid=(b, 2),
        in_specs=in_specs,
        out_specs=pl.BlockSpec((None, seq, LANES), lambda i, hp: (i, 0, hp)),
        out_shape=jax.ShapeDtypeStruct((b, seq, A_OUT), BF16),
        scratch_shapes=[pltpu.VMEM((A_GROUPS, seq, LANES), F32), pltpu.VMEM((A_GROUPS, seq, LANES), F32),
                        ] + [pltpu.VMEM((seq, LANES), F32)] * 5,
        compiler_params=_params("parallel", "parallel"),
        name="attn_a",
    )(*([za] * 9))


NSA_TQ = 256
NSA_TK = 512
KEY_BLK = 128


def _softmax_step_t(s, vt, m, acc, mask=None):
    if mask is not None:
        s = jnp.where(mask, s, NEG)
    m_new = jnp.maximum(m, jnp.max(s, axis=0, keepdims=True))
    acc = jnp.exp2(m - m_new) * acc + _dot(vt, jnp.exp2(s - m_new).astype(BF16))
    return m_new, acc


def _nsa_kernel(q_ref, kv_ref, kvc_ref, zg_ref, ovt_ref, o_ref, kaug_ref, vst_ref, vwt_ref):
    seq = kv_ref.shape[0]
    nsel = seq // SLC_BLOCK
    ncmp = kvc_ref.shape[0]
    tq = NSA_TQ
    mcols = B_GROUP * tq
    qi = pl.program_id(2)
    t0 = qi * tq
    upper = lax.broadcasted_iota(jnp.int32, (LANES, KEY_BLK), 0) < HEAD_DIM

    @pl.when(qi == 0)
    def _():
        key = lax.broadcasted_iota(jnp.int32, (seq, LANES), 0)
        lane = lax.broadcasted_iota(jnp.int32, (seq, LANES), 1)
        onehot = jnp.logical_and(lane >= HEAD_DIM, (key // SLC_BLOCK) == lane - HEAD_DIM)
        kaug_ref[...] = jnp.where(lane < HEAD_DIM, kv_ref[:, :LANES],
                                  jnp.where(onehot, 1.0, 0.0).astype(BF16))
        for kb in range(seq // KEY_BLK):
            rows = slice(kb * KEY_BLK, (kb + 1) * KEY_BLK)
            vst_ref[kb] = jnp.where(upper, 1.0, kv_ref[rows, :LANES].astype(F32).T).astype(BF16)
            vwt_ref[kb] = jnp.where(upper, 1.0, kv_ref[rows, LANES:].astype(F32).T).astype(BF16)

    lane_q = lax.broadcasted_iota(jnp.int32, (tq, LANES), 1)
    low = lane_q < HEAD_DIM
    qf = q_ref[...].astype(F32) * (SCALE * LOG2E)
    parts = []
    for pair in range(2):
        x = qf[:, pair * LANES:(pair + 1) * LANES]
        parts.append(jnp.where(low, x, 0.0))
        parts.append(jnp.where(low, pltpu.roll(x, HEAD_DIM, 1), 0.0))
    q128 = jnp.concatenate(parts, axis=0)
    qw = q128.astype(BF16)

    def positions(nkeys, key0):
        kpos = key0 + lax.broadcasted_iota(jnp.int32, (nkeys, mcols), 0)
        tpos = t0 + lax.broadcasted_iota(jnp.int32, (nkeys, mcols), 1) % tq
        return kpos, tpos

    init = (jnp.full((1, mcols), NEG, F32), jnp.zeros((LANES, mcols), F32))

    kvc = kvc_ref[...]
    s = _dot_nt(kvc.astype(BF16), qw)
    cidx, tpos = positions(ncmp, 0)
    s = jnp.where(cidx * CMP_STRIDE + (CMP_LEN - 1) <= tpos, s, -jnp.inf)
    m = jnp.max(s, axis=0, keepdims=True)
    m = jnp.where(m == -jnp.inf, 0.0, m)
    e = jnp.exp2(s - m)
    l_c = jnp.maximum(jnp.sum(e, axis=0, keepdims=True), 1e-30)
    p = e / l_c
    acc_c = _dot(kvc.T.astype(BF16), p.astype(BF16))

    psum = p[:, 0:tq] + p[:, tq:2 * tq] + p[:, 2 * tq:3 * tq] + p[:, 3 * tq:4 * tq]
    p_hi = psum.astype(BF16)
    r1 = psum - p_hi.astype(F32)
    p_mid = r1.astype(BF16)
    p_lo = (r1 - p_mid.astype(F32)).astype(BF16)
    ovt = ovt_ref[...]
    imp = _dot(ovt, p_hi) + _dot(ovt, p_mid) + _dot(ovt, p_lo)
    blk = lax.broadcasted_iota(jnp.int32, (nsel, tq), 0)
    tcol = t0 + lax.broadcasted_iota(jnp.int32, (nsel, tq), 1)
    cur = tcol // SLC_BLOCK
    forced = jnp.logical_or(blk == 0, jnp.logical_or(blk == cur, blk == cur - 1))
    future = blk * SLC_BLOCK > tcol
    imp = jnp.where(future, -jnp.inf, jnp.where(forced, jnp.inf, imp))
    rank = jnp.zeros((nsel, tq), F32)
    for mp in range(nsel):
        other = imp[mp:mp + 1, :]
        before = jnp.logical_or(other > imp, jnp.logical_and(other == imp, blk > mp))
        rank = rank + jnp.where(before, 1.0, 0.0)
    bias_t = jnp.where(rank < float(min(SLC_TOPK, nsel)), 0.0, NEG)
    bias_t = jnp.concatenate([jnp.zeros((HEAD_DIM, tq), F32), bias_t,
                              jnp.zeros((LANES - HEAD_DIM - nsel, tq), F32)], axis=0)
    bias = bias_t.T
    qs = (q128 + jnp.concatenate([bias] * B_GROUP, axis=0)).astype(BF16)

    def values_t(ref, first_blk, nblk):
        return jnp.concatenate([ref[first_blk + u] for u in range(nblk)], axis=1)

    diag = pl.ds(pl.multiple_of(t0, tq), tq)
    own_r, own_c = positions(tq, t0)
    not_after = own_r <= own_c

    def ref_score(k_own, qop):
        return jnp.max(jnp.where(not_after, _dot_nt(k_own, qop), NEG), axis=0, keepdims=True)

    def weighted_values(s, vt, m_ref, mask):
        if mask is not None:
            s = jnp.where(mask, s, NEG)
        return _dot(vt, jnp.exp2(s - m_ref).astype(BF16))

    w0 = jnp.maximum(t0 - WIN, 0)
    wrows = pl.ds(pl.multiple_of(w0, tq), WIN + tq)
    kpos, tpos = positions(WIN + tq, w0)
    in_window = jnp.logical_and(kpos <= tpos, tpos - kpos < WIN)
    acc_w = weighted_values(_dot_nt(kv_ref[wrows, LANES:], qw),
                            values_t(vwt_ref, w0 // KEY_BLK, (WIN + tq) // KEY_BLK),
                            ref_score(kv_ref[diag, LANES:], qw), in_window)

    def key_rows(j):
        return pl.ds(pl.multiple_of(j * NSA_TK, NSA_TK), NSA_TK)

    m_ref = ref_score(kaug_ref[diag, :], qs)
    nblk = NSA_TK // KEY_BLK

    def slc_tile(j, mask):
        return weighted_values(_dot_nt(kaug_ref[j * NSA_TK:(j + 1) * NSA_TK, :], qs),
                               values_t(vst_ref, j * nblk, nblk), m_ref, mask)

    def slc_tiles(ntiles):
        def run():
            kpos, tpos = positions(NSA_TK, (ntiles - 1) * NSA_TK)
            acc = slc_tile(ntiles - 1, kpos <= tpos)
            for j in range(ntiles - 1):
                acc = acc + slc_tile(j, None)
            return acc
        return run

    acc_s = lax.switch(t0 // NSA_TK, [slc_tiles(n + 1) for n in range(seq // NSA_TK)])

    gate_t = jax.nn.sigmoid(zg_ref[...]).T

    def grow(br):
        return jnp.concatenate([gate_t[3 * g + br:3 * g + br + 1, :] for g in range(B_GROUP)], axis=1)

    def gated_sum(a_s, a_w):
        return acc_c * grow(0) + a_s * (grow(1) / a_s[0:1, :]) + a_w * (grow(2) / a_w[0:1, :])

    def running_max_path():
        _, a_w = _softmax_step_t(_dot_nt(kv_ref[wrows, LANES:], qw),
                                 values_t(vwt_ref, w0 // KEY_BLK, (WIN + tq) // KEY_BLK), *init, mask=in_window)

        def body(j, carry):
            kpos, tpos = positions(NSA_TK, j * NSA_TK)
            return _softmax_step_t(_dot_nt(kaug_ref[key_rows(j), :], qs), values_t(vst_ref, j * nblk, nblk),
                                   *carry, mask=kpos <= tpos)

        _, a_s = lax.fori_loop(0, t0 // NSA_TK + 1, body, init)
        return gated_sum(a_s, a_w)

    y_t = gated_sum(acc_s, acc_w)
    finite = jnp.logical_and(jnp.isfinite(y_t[HEAD_DIM:, :]),
                             jnp.isfinite(acc_s[0:1, :] + acc_w[0:1, :]))
    overflowed = jnp.max(jnp.where(finite, 0.0, 1.0)) > 0.0
    y_t = lax.cond(overflowed, running_max_path, lambda: y_t)
    for pair in range(2):
        even = y_t[:, (2 * pair) * tq:(2 * pair + 1) * tq].T
        odd = y_t[:, (2 * pair + 1) * tq:(2 * pair + 2) * tq].T
        o_ref[:, pair * LANES:(pair + 1) * LANES] = jnp.where(
            low, pltpu.roll(even, HEAD_DIM, 1), odd).astype(o_ref.dtype)


def _nsa(zq, zkv, kvc, zg, ovt):
    b, seq, _ = zq.shape
    ncmp = kvc.shape[2]
    nsel = seq // SLC_BLOCK
    return pl.pallas_call(
        _nsa_kernel,
        grid=(b, B_KV_HEADS, seq // NSA_TQ),
        in_specs=[pl.BlockSpec((None, NSA_TQ, MXU_DIM), lambda i, h, q: (i, q, h)),
                  pl.BlockSpec((None, seq, MXU_DIM), lambda i, h, q: (i, 0, h)),
                  pl.BlockSpec((None, None, ncmp, LANES), lambda i, h, q: (i, h, 0, 0)),
                  pl.BlockSpec((None, NSA_TQ, LANES), lambda i, h, q: (i, q, h)),
                  pl.BlockSpec((nsel, ncmp), lambda i, h, q: (0, 0))],
        out_specs=pl.BlockSpec((None, NSA_TQ, MXU_DIM), lambda i, h, q: (i, q, h)),
        out_shape=jax.ShapeDtypeStruct((b, seq, B_Q), BF16),
        scratch_shapes=[pltpu.VMEM((seq, LANES), BF16)]
        + [pltpu.VMEM((seq // KEY_BLK, LANES, KEY_BLK), BF16)] * 2,
        compiler_params=_params("parallel", "parallel", "arbitrary"),
        name="nsa",
    )(zq, zkv, kvc, zg, ovt)


def _merge_kernel(h_ref, ya_ref, yb_ref, wg_ref, wba_ref, wbb_ref, wmix_ref, g_ref, b_ref, o_ref):
    h = h_ref[...]
    hb = h.astype(BF16)
    ya = ya_ref[...]
    yb = yb_ref[...]
    nchunk = D_MODEL // MXU_DIM

    def pre(n):
        cols = slice(n * MXU_DIM, (n + 1) * MXU_DIM)
        gcols = slice(D_MODEL + n * MXU_DIM, D_MODEL + (n + 1) * MXU_DIM)
        return (_dot(hb, wg_ref[:, cols]), _dot(ya, wba_ref[:, cols]),
                _dot(hb, wg_ref[:, gcols]), _dot(yb, wbb_ref[:, cols]))

    ahead = [pre(n) for n in range(2)]
    mix = jnp.zeros(h.shape, F32)
    for n in range(nchunk):
        za, pa, zb, pb = ahead.pop(0)
        if n + 2 < nchunk:
            ahead.append(pre(n + 2))
        merged = jax.nn.sigmoid(za) * pa + jax.nn.sigmoid(zb) * pb
        mix = mix + _dot(merged.astype(BF16), wmix_ref[n * MXU_DIM:(n + 1) * MXU_DIM, :])
    o_ref[...] = _layer_norm(ALPHA * h + mix, g_ref[...], b_ref[...])


def _merge(h2d, ya, yb, wg, wba, wbb, wmix, g, b, tm):
    t = h2d.shape[0]
    row = lambda i: (i, 0)
    const = lambda i: (0, 0)
    return pl.pallas_call(
        _merge_kernel,
        grid=(t // tm,),
        in_specs=[pl.BlockSpec((tm, D_MODEL), row), pl.BlockSpec((tm, A_OUT), row),
                  pl.BlockSpec((tm, B_Q), row), pl.BlockSpec((D_MODEL, 2 * D_MODEL), const),
                  pl.BlockSpec((A_OUT, D_MODEL), const), pl.BlockSpec((B_Q, D_MODEL), const),
                  pl.BlockSpec((D_MODEL, D_MODEL), const), pl.BlockSpec((1, D_MODEL), const),
                  pl.BlockSpec((1, D_MODEL), const)],
        out_specs=pl.BlockSpec((tm, D_MODEL), row),
        out_shape=jax.ShapeDtypeStruct((t, D_MODEL), F32),
        compiler_params=_params("parallel"),
        name="merge",
    )(h2d, ya, yb, wg, wba, wbb, wmix, g, b)


def _xkv_kernel(mem_ref, w_ref, o_ref):
    o_ref[...] = _dot(mem_ref[...].astype(BF16), w_ref[...]).astype(o_ref.dtype)


def _xkv(mem, w):
    b, mlen, _ = mem.shape
    return pl.pallas_call(
        _xkv_kernel,
        grid=(b,),
        in_specs=[pl.BlockSpec((None, mlen, D_MODEL), lambda i: (i, 0, 0)),
                  pl.BlockSpec((D_MODEL, 2 * X_DIM), lambda i: (0, 0))],
        out_specs=pl.BlockSpec((None, mlen, 2 * X_DIM), lambda i: (i, 0, 0)),
        out_shape=jax.ShapeDtypeStruct((b, mlen, 2 * X_DIM), BF16),
        compiler_params=_params("parallel"),
        name="xkv",
    )(mem, w)


def _xattn_kernel(h_ref, kv_ref, wq_ref, wo_ref, g_ref, b_ref, o_ref):
    h = h_ref[...]
    tm = h.shape[0]
    q = _dot(h.astype(BF16), wq_ref[...]) * SCALE
    k = kv_ref[:, :X_DIM]
    v = kv_ref[:, X_DIM:]
    lane = lax.broadcasted_iota(jnp.int32, (tm, X_DIM), 1)
    o = jnp.zeros((tm, X_DIM), F32)
    for hd in range(X_HEADS):
        in_head = (lane // HEAD_DIM) == hd
        s = _dot_nt(jnp.where(in_head, q, 0.0).astype(BF16), k)
        m = jnp.max(s, axis=-1, keepdims=True)
        e = jnp.exp(s - m)
        p = e / jnp.sum(e, axis=-1, keepdims=True)
        o = jnp.where(in_head, _dot(p.astype(BF16), v), o)
    att = _dot(o.astype(BF16), wo_ref[...])
    o_ref[...] = _layer_norm(ALPHA * h + att, g_ref[...], b_ref[...])


def _xattn(h2d, kv, wq, wo, g, b, seq, tm):
    t = h2d.shape[0]
    mlen = kv.shape[1]
    nseq = seq // tm
    row = lambda i: (i, 0)
    const = lambda i: (0, 0)
    return pl.pallas_call(
        _xattn_kernel,
        grid=(t // tm,),
        in_specs=[pl.BlockSpec((tm, D_MODEL), row),
                  pl.BlockSpec((None, mlen, 2 * X_DIM), lambda i: (i // nseq, 0, 0)),
                  pl.BlockSpec((D_MODEL, X_DIM), const), pl.BlockSpec((X_DIM, D_MODEL), const),
                  pl.BlockSpec((1, D_MODEL), const), pl.BlockSpec((1, D_MODEL), const)],
        out_specs=pl.BlockSpec((tm, D_MODEL), row),
        out_shape=jax.ShapeDtypeStruct((t, D_MODEL), F32),
        compiler_params=_params("parallel"),
        name="xattn",
    )(h2d, kv, wq, wo, g, b)


FFN_CHUNK = 256
FFN_AHEAD = 2
CARRY_ROWS = SUBLANES


def _ffn_kernel(h_ref, wup_ref, cw_ref, cb_ref, wdn_ref, g_ref, b_ref, o_ref, u_sc, *, tiles_per_seq):
    tm = h_ref.shape[0]
    i = pl.program_id(0)

    @pl.when(i % tiles_per_seq == 0)
    def _():
        u_sc[0:CARRY_ROWS, :] = jnp.zeros((CARRY_ROWS, 2 * D_FF), F32)

    h = h_ref[...]
    hb = h.astype(BF16)
    acc = jnp.zeros((tm, D_MODEL), F32)
    nchunk = D_FF // FFN_CHUNK

    def up(c):
        return [_dot(hb, wup_ref[:, base:base + FFN_CHUNK]) for base in (c * FFN_CHUNK, D_FF + c * FFN_CHUNK)]

    ahead = [up(c) for c in range(FFN_AHEAD)]
    for c in range(nchunk):
        u_cur = ahead.pop(0)
        if c + FFN_AHEAD < nchunk:
            ahead.append(up(c + FFN_AHEAD))
        halves = []
        for u, base in zip(u_cur, (c * FFN_CHUNK, D_FF + c * FFN_CHUNK)):
            cols = slice(base, base + FFN_CHUNK)
            u_sc[CARRY_ROWS:CARRY_ROWS + tm, cols] = u
            y = (cw_ref[2:3, cols] * u
                 + cw_ref[1:2, cols] * u_sc[CARRY_ROWS - 1:CARRY_ROWS - 1 + tm, cols]
                 + cw_ref[0:1, cols] * u_sc[CARRY_ROWS - 2:CARRY_ROWS - 2 + tm, cols]
                 + cb_ref[:, cols])
            u_sc[CARRY_ROWS - 2:CARRY_ROWS, cols] = u_sc[CARRY_ROWS + tm - 2:CARRY_ROWS + tm, cols]
            halves.append(y)
        act = jax.nn.gelu(halves[0]) * halves[1]
        acc = acc + _dot(act.astype(BF16), wdn_ref[c * FFN_CHUNK:(c + 1) * FFN_CHUNK, :])
    o_ref[...] = _layer_norm(ALPHA * h + acc, g_ref[...], b_ref[...])


def _ffn(h2d, wup, cw, cb, wdn, g, b, seq, tm):
    t = h2d.shape[0]
    row = lambda i: (i, 0)
    const = lambda i: (0, 0)
    return pl.pallas_call(
        functools.partial(_ffn_kernel, tiles_per_seq=seq // tm),
        grid=(t // tm,),
        in_specs=[pl.BlockSpec((tm, D_MODEL), row), pl.BlockSpec((D_MODEL, 2 * D_FF), const),
                  pl.BlockSpec((CONV_W, 2 * D_FF), const), pl.BlockSpec((1, 2 * D_FF), const),
                  pl.BlockSpec((D_FF, D_MODEL), const), pl.BlockSpec((1, D_MODEL), const),
                  pl.BlockSpec((1, D_MODEL), const)],
        out_specs=pl.BlockSpec((tm, D_MODEL), row),
        out_shape=jax.ShapeDtypeStruct((t, D_MODEL), F32),
        scratch_shapes=[pltpu.VMEM((CARRY_ROWS + tm, 2 * D_FF), F32)],
        compiler_params=_params("arbitrary"),
        name="ffn",
    )(h2d, wup, cw, cb, wdn, g, b)


def _rope_tables(seq):
    inv = ROPE_THETA ** (-jnp.arange(HALF, dtype=F32) / HALF)
    ang = jnp.arange(seq, dtype=F32)[:, None] * inv[None, :]
    cos, sin = jnp.cos(ang), jnp.sin(ang)
    zero, one = jnp.zeros_like(cos), jnp.ones_like(cos)
    cq = jnp.concatenate([cos, cos, cos, cos], axis=1)
    s1q = jnp.concatenate([-sin, zero, -sin, zero], axis=1)
    s2q = jnp.concatenate([zero, sin, zero, sin], axis=1)
    ck = jnp.concatenate([cos, cos, one, one], axis=1)
    s1k = jnp.concatenate([-sin, zero, zero, zero], axis=1)
    s2k = jnp.concatenate([zero, sin, zero, zero], axis=1)
    pos_c = (CMP_STRIDE * jnp.arange(seq // CMP_STRIDE) + CMP_LEN - 1).astype(F32)
    ang_c = pos_c[:, None] * inv[None, :]
    return (cq, s1q, s2q, ck, s1k, s2k), jnp.cos(ang_c), jnp.sin(ang_c)


def _overlap_t(seq):
    ncmp = seq // CMP_STRIDE
    nsel = seq // SLC_BLOCK
    c_start = CMP_STRIDE * jnp.arange(ncmp)
    s_start = SLC_BLOCK * jnp.arange(nsel)
    ov = jnp.clip(jnp.minimum(c_start[None, :] + CMP_LEN, s_start[:, None] + SLC_BLOCK)
                  - jnp.maximum(c_start[None, :], s_start[:, None]), 0).astype(F32) / CMP_LEN
    return ov.astype(BF16)


def _pack_w_in(w):
    o1 = A_QKV
    o2 = o1 + B_Q
    o3 = o2 + B_KV
    o4 = o3 + B_GATE
    wa, wq, wkv, wbg, wgate = w[:, :o1], w[:, o1:o2], w[:, o2:o3], w[:, o3:o4], w[:, o4:]
    kv = wkv.reshape(D_MODEL, 3, 2, B_KV_HEADS, HEAD_DIM)
    per_head = [jnp.concatenate([kv[:, 1, 0, hh], kv[:, 1, 1, hh], kv[:, 2, 0, hh], kv[:, 2, 1, hh]], axis=1)
                for hh in range(B_KV_HEADS)]
    cmp_blk = wkv[:, :2 * B_KV_HEADS * HEAD_DIM]
    per_gate = B_GROUP * 3
    gates = [jnp.pad(wbg[:, hh * per_gate:(hh + 1) * per_gate], ((0, 0), (0, LANES - per_gate)))
             for hh in range(B_KV_HEADS)]
    wp = jnp.concatenate([wa, wq] + per_head + [cmp_blk] + gates, axis=1)
    return wp.astype(BF16), wgate.astype(BF16)


def kernel(x, mem, w_in, cmp_pos, cmp_w1, cmp_b1, cmp_w2, w_branch_a, w_branch_b, w_mix_out, ln1_g, ln1_b,
           w_xq, w_xkv, w_xo, ln2_g, ln2_b, w_up, conv_w, conv_b, w_down, ln3_g, ln3_b):
    b, seq, _ = x.shape
    t = b * seq
    tabs, cos_c, sin_c = _rope_tables(seq)
    ovt = _overlap_t(seq)
    h = x.reshape(t, D_MODEL)
    for l in range(DEPTH):
        wp, wgate = _pack_w_in(w_in[l])
        za, zq, zkv, zc, zg = _project(h, wp, tabs, seq, tm=512)
        kvc = _compress(zc.reshape(b, seq, MXU_DIM),
                        cmp_w1[l].reshape(2, CMP_LEN, HEAD_DIM, CMP_HIDDEN).astype(BF16),
                        cmp_pos[l], cmp_b1[l], cmp_w2[l].astype(BF16), cos_c, sin_c)
        ya = _attn_a(za.reshape(b, seq, A_QKV))
        yb = _nsa(zq.reshape(b, seq, B_Q), zkv.reshape(b, seq, 2 * MXU_DIM), kvc,
                  zg.reshape(b, seq, MXU_DIM), ovt)
        h = _merge(h, ya.reshape(t, A_OUT), yb.reshape(t, B_Q), wgate, w_branch_a[l].astype(BF16),
                   w_branch_b[l].astype(BF16), w_mix_out[l].astype(BF16),
                   ln1_g[l][None, :], ln1_b[l][None, :], tm=256)
        xkv = _xkv(mem, w_xkv[l].astype(BF16))
        h = _xattn(h, xkv, w_xq[l].astype(BF16), w_xo[l].astype(BF16),
                   ln2_g[l][None, :], ln2_b[l][None, :], seq, tm=512)
        h = _ffn(h, w_up[l].astype(BF16), conv_w[l], conv_b[l][None, :], w_down[l].astype(BF16),
                 ln3_g[l][None, :], ln3_b[l][None, :], seq, tm=256)
    return h.reshape(b, seq, D_MODEL)
```

```python
import functools

import jax
import jax.numpy as jnp
from jax import lax
from jax.experimental import pallas as pl
from jax.experimental.pallas import tpu as pltpu

F32 = jnp.float32
BF16 = jnp.bfloat16

D_MODEL = 1024
DEPTH = 2
HEAD_DIM = 64
HALF = HEAD_DIM // 2
ROPE_THETA = 10000.0
LN_EPS = 1e-5
ALPHA = (2 * DEPTH) ** 0.25
SCALE = HEAD_DIM ** -0.5

DIL_PATTERNS = ((128, 1), (512, 4), (2048, 16))
A_GROUPS = 3
A_HEADS_PER_GROUP = 4
A_QKV = 3 * A_GROUPS * A_HEADS_PER_GROUP * HEAD_DIM
A_OUT = A_HEADS_PER_GROUP * HEAD_DIM

B_Q_HEADS = 8
B_KV_HEADS = 2
B_GROUP = 4
B_Q = B_Q_HEADS * HEAD_DIM
B_KV = 3 * 2 * B_KV_HEADS * HEAD_DIM
B_GATE = 3 * B_Q_HEADS
CMP_STRIDE = 16
CMP_LEN = 32
CMP_HIDDEN = 256
SLC_BLOCK = 64
SLC_TOPK = 8
WIN = 512

X_HEADS = 4
X_DIM = X_HEADS * HEAD_DIM
D_FF = 2816
CONV_W = 3

LANES = 128
SUBLANES = 8
MXU_DIM = 256
VMEM_LIMIT_BYTES = 56 * 1024 * 1024

NEG = -1e30
LOG2E = 1.4426950408889634

N_PROJ_BLOCKS = 15
PROJ_COLS = N_PROJ_BLOCKS * MXU_DIM
PROJ_AHEAD = 2

_NT = (((1,), (1,)), ((), ()))


def _dot(a, b):
    return jnp.dot(a, b, preferred_element_type=F32)


def _dot_nt(a, b):
    return lax.dot_general(a, b, _NT, preferred_element_type=F32)


def _params(*sem):
    return pltpu.CompilerParams(dimension_semantics=sem, vmem_limit_bytes=VMEM_LIMIT_BYTES)


def _layer_norm(x, g, b):
    mu = jnp.mean(x, axis=-1, keepdims=True)
    xc = x - mu
    var = jnp.mean(xc * xc, axis=-1, keepdims=True)
    return xc * lax.rsqrt(var + LN_EPS) * g + b


def _gelu_tanh(x):
    c = 0.7978845608028654
    half = 0.5 * x
    return half + half * jnp.tanh(x * (c + (c * 0.044715) * (x * x)))


def _rope128(z, c, s1, s2):
    return z * c + pltpu.roll(z, LANES - HALF, 1) * s1 + pltpu.roll(z, HALF, 1) * s2


def _proj_kernel(x_ref, w_ref, cq_ref, s1q_ref, s2q_ref, ck_ref, s1k_ref, s2k_ref,
                 za_ref, zq_ref, zkv_ref, zc_ref, zg_ref):
    xb = x_ref[...].astype(BF16)

    def block(j):
        return _dot(xb, w_ref[:, j * MXU_DIM:(j + 1) * MXU_DIM])

    ahead = [block(j) for j in range(PROJ_AHEAD)]
    for j in range(N_PROJ_BLOCKS):
        z = ahead.pop(0)
        if j + PROJ_AHEAD < N_PROJ_BLOCKS:
            ahead.append(block(j + PROJ_AHEAD))
        if j < 6 or 9 <= j <= 10:
            tabs = (cq_ref[...], s1q_ref[...], s2q_ref[...])
        elif 11 <= j <= 12:
            tabs = (ck_ref[...], s1k_ref[...], s2k_ref[...])
        else:
            tabs = None
        if tabs is not None:
            z = jnp.concatenate([_rope128(z[:, :LANES], *tabs), _rope128(z[:, LANES:], *tabs)], axis=1)
        if j < 9:
            za_ref[:, j * MXU_DIM:(j + 1) * MXU_DIM] = z
        elif j < 11:
            zq_ref[:, (j - 9) * MXU_DIM:(j - 8) * MXU_DIM] = z.astype(BF16)
        elif j < 13:
            zkv_ref[:, (j - 11) * MXU_DIM:(j - 10) * MXU_DIM] = z.astype(BF16)
        elif j == 13:
            zc_ref[...] = z
        else:
            zg_ref[...] = z


def _project(h2d, w, tabs, seq, tm):
    t = h2d.shape[0]
    nseq = seq // tm
    row = lambda i: (i, 0)
    tab_spec = pl.BlockSpec((tm, LANES), lambda i: (i % nseq, 0))
    return pl.pallas_call(
        _proj_kernel,
        grid=(t // tm,),
        in_specs=[pl.BlockSpec((tm, D_MODEL), row),
                  pl.BlockSpec((D_MODEL, PROJ_COLS), lambda i: (0, 0))] + [tab_spec] * 6,
        out_specs=[pl.BlockSpec((tm, A_QKV), row), pl.BlockSpec((tm, B_Q), row),
                   pl.BlockSpec((tm, 2 * MXU_DIM), row), pl.BlockSpec((tm, MXU_DIM), row),
                   pl.BlockSpec((tm, MXU_DIM), row)],
        out_shape=[jax.ShapeDtypeStruct((t, A_QKV), F32), jax.ShapeDtypeStruct((t, B_Q), BF16),
                   jax.ShapeDtypeStruct((t, 2 * MXU_DIM), BF16), jax.ShapeDtypeStruct((t, MXU_DIM), F32),
                   jax.ShapeDtypeStruct((t, MXU_DIM), F32)],
        compiler_params=_params("parallel"),
        name="proj",
    )(h2d, w, *tabs)


def _cmp_kernel(zc_ref, w1_ref, pe_ref, b1_ref, w2_ref, cos_ref, sin_ref, o_ref, xk_ref, xv_ref):
    seq = zc_ref.shape[0]
    nblk = seq // CMP_STRIDE
    xs_refs = (xk_ref, xv_ref)
    for kv in range(2):
        xs_refs[kv][0:seq, :] = zc_ref[:, kv * LANES:(kv + 1) * LANES]
        xs_refs[kv][seq:seq + CMP_LEN, :] = jnp.zeros((CMP_LEN, LANES), F32)
    acc = [jnp.zeros((nblk, CMP_HIDDEN), F32) for _ in range(4)]
    for j in range(CMP_LEN):
        xj = [r[pl.ds(j, nblk, stride=CMP_STRIDE), :] for r in xs_refs]
        for c in range(4):
            kv, hh = c // 2, c % 2
            piece = xj[kv][:, hh * HEAD_DIM:(hh + 1) * HEAD_DIM] + pe_ref[kv, j:j + 1, :]
            acc[c] = acc[c] + _dot(piece.astype(BF16), w1_ref[kv, j])
    outs = []
    for c in range(4):
        kv = c // 2
        hid = jax.nn.gelu(acc[c] + b1_ref[kv:kv + 1, :])
        y = _dot(hid.astype(BF16), w2_ref[kv])
        if kv == 0:
            y1, y2 = y[:, :HALF], y[:, HALF:]
            cs, sn = cos_ref[...], sin_ref[...]
            y = jnp.concatenate([y1 * cs - y2 * sn, y1 * sn + y2 * cs], axis=1)
        outs.append(y)
    o_ref[0] = jnp.concatenate([outs[0], outs[2]], axis=1)
    o_ref[1] = jnp.concatenate([outs[1], outs[3]], axis=1)


def _compress(zc, w1, pe, b1, w2, cos_c, sin_c):
    b, seq, _ = zc.shape
    nblk = seq // CMP_STRIDE
    full = lambda *shape: pl.BlockSpec(shape, lambda i: (0,) * len(shape))
    return pl.pallas_call(
        _cmp_kernel,
        grid=(b,),
        in_specs=[pl.BlockSpec((None, seq, MXU_DIM), lambda i: (i, 0, 0)),
                  full(2, CMP_LEN, HEAD_DIM, CMP_HIDDEN), full(2, CMP_LEN, HEAD_DIM),
                  full(2, CMP_HIDDEN), full(2, CMP_HIDDEN, HEAD_DIM),
                  full(nblk, HALF), full(nblk, HALF)],
        out_specs=pl.BlockSpec((None, 2, nblk, LANES), lambda i: (i, 0, 0, 0)),
        out_shape=jax.ShapeDtypeStruct((b, 2, nblk, LANES), F32),
        scratch_shapes=[pltpu.VMEM((seq + CMP_LEN, LANES), F32), pltpu.VMEM((seq + CMP_LEN, LANES), F32)],
        compiler_params=_params("parallel"),
        name="compress",
    )(zc, w1, pe, b1, w2, cos_c, sin_c)


A_TILE = 128
A_UNROLL = 8
FREE_STRIDE = 4


def _attn_a_kernel(q0, k0, v0, q1, k1, v1, q2, k2, v2, y_ref, o_sc, l_sc, tq_sc, tk_sc, tv_sc, to_sc, tl_sc):
    seq = y_ref.shape[0]
    qkv = ((q0, k0, v0), (q1, k1, v1), (q2, k2, v2))
    first_head = lax.broadcasted_iota(jnp.int32, (A_TILE, LANES), 1) < HEAD_DIM

    for g, (_, dil) in enumerate(DIL_PATTERNS):
        q_ref, k_ref, v_ref = qkv[g]
        o_dst, l_dst = o_sc.at[g], l_sc.at[g]
        ntile = seq // dil // A_TILE
        span = A_TILE * dil
        nkeys = 2 * A_TILE if ntile > 1 else A_TILE
        row = lax.broadcasted_iota(jnp.int32, (A_TILE, nkeys), 0)
        col = lax.broadcasted_iota(jnp.int32, (A_TILE, nkeys), 1)
        band = jnp.logical_and(col >= row, col <= row + A_TILE) if ntile > 1 else col <= row
        own = col >= A_TILE
        two_stage = dil > FREE_STRIDE
        if two_stage:
            assert ntile == 1 and dil % FREE_STRIDE == 0
            part = seq // FREE_STRIDE
            srcs = (q_ref, k_ref, v_ref)
            q_ref, k_ref, v_ref, o_dst, l_dst = tq_sc, tk_sc, tv_sc, to_sc, tl_sc
            for src, dst in zip(srcs, (q_ref, k_ref, v_ref)):
                for r4 in range(FREE_STRIDE):
                    dst[r4 * part:(r4 + 1) * part, :] = src[pl.ds(r4, part, stride=FREE_STRIDE), :]

        def rows_at(start, dil=dil, two_stage=two_stage):
            if dil == 1:
                return pl.ds(pl.multiple_of(start, A_TILE), A_TILE)
            if two_stage:
                return pl.ds((start % FREE_STRIDE) * (seq // FREE_STRIDE) + start // FREE_STRIDE, A_TILE,
                             stride=dil // FREE_STRIDE)
            return pl.ds(start, A_TILE, stride=dil)

        def load(n, q_ref=q_ref, k_ref=k_ref, v_ref=v_ref, ntile=ntile, span=span, rows_at=rows_at,
                 band=band, own=own):
            r = n // ntile
            i = n % ntile
            start = r + i * span
            rows = rows_at(start)
            q = q_ref[rows, :] * (SCALE * LOG2E)
            q2 = jnp.concatenate([jnp.where(first_head, q, 0.0), jnp.where(first_head, 0.0, q)], axis=0).astype(BF16)
            k = k_ref[rows, :].astype(BF16)
            v = v_ref[rows, :].astype(BF16)
            valid = band
            if ntile > 1:
                prows = rows_at(jnp.maximum(start - span, r))
                k = jnp.concatenate([k_ref[prows, :].astype(BF16), k], axis=0)
                v = jnp.concatenate([v_ref[prows, :].astype(BF16), v], axis=0)
                valid = jnp.logical_and(band, jnp.logical_or(own, i > 0))
            return rows, q2, k, v, jnp.concatenate([valid, valid], axis=0)

        def body(it, carry, load=load, o_dst=o_dst, l_dst=l_dst):
            tiles = [load(it * A_UNROLL + u) for u in range(A_UNROLL)]
            scores = [jnp.where(valid, _dot_nt(q2, k), NEG) for _, q2, k, _, valid in tiles]
            for (rows, _, _, v, _), s in zip(tiles, scores):
                m = jnp.max(s, axis=-1, keepdims=True)
                e = jnp.exp2(s - m)
                l = jnp.sum(e, axis=-1, keepdims=True)
                o = _dot(e.astype(BF16), v) / l
                lse = jnp.broadcast_to(jnp.log2(l) + m, o.shape)
                o_dst[rows, :] = jnp.where(first_head, o[:A_TILE], o[A_TILE:])
                l_dst[rows, :] = jnp.where(first_head, lse[:A_TILE], lse[A_TILE:])
            return carry

        lax.fori_loop(0, seq // A_TILE // A_UNROLL, body, 0)
        if two_stage:
            for src, dst in ((o_dst, o_sc.at[g]), (l_dst, l_sc.at[g])):
                for r4 in range(FREE_STRIDE):
                    dst[pl.ds(r4, part, stride=FREE_STRIDE), :] = src[r4 * part:(r4 + 1) * part, :]

    chunk = 256
    for c in range(seq // chunk):
        rows = pl.ds(c * chunk, chunk)
        ls = [l_sc[g, rows, :] for g in range(A_GROUPS)]
        mx = jnp.maximum(jnp.maximum(ls[0], ls[1]), ls[2])
        ws = [jnp.exp2(l - mx) for l in ls]
        num = ws[0] * o_sc[0, rows, :] + ws[1] * o_sc[1, rows, :] + ws[2] * o_sc[2, rows, :]
        y_ref[rows, :] = (num / (ws[0] + ws[1] + ws[2])).astype(y_ref.dtype)


def _attn_a(za):
    b, seq, _ = za.shape
    in_specs = []
    for g in range(A_GROUPS):
        for part in range(3):
            blk = part * 2 * A_GROUPS + 2 * g
            in_specs.append(pl.BlockSpec((None, seq, LANES), lambda i, hp, blk=blk: (i, 0, blk + hp)))
    return pl.pallas_call(
        _attn_a_kernel,
        grid=(b, 2),
        in_specs=in_specs,
        out_specs=pl.BlockSpec((None, seq, LANES), lambda i, hp: (i, 0, hp)),
        out_shape=jax.ShapeDtypeStruct((b, seq, A_OUT), BF16),
        scratch_shapes=[pltpu.VMEM((A_GROUPS, seq, LANES), F32), pltpu.VMEM((A_GROUPS, seq, LANES), F32),
                        ] + [pltpu.VMEM((seq, LANES), F32)] * 5,
        compiler_params=_params("parallel", "parallel"),
        name="attn_a",
    )(*([za] * 9))


NSA_TQ = 256
NSA_TK = 512
KEY_BLK = 128


def _softmax_step_t(s, vt, m, acc, mask=None):
    if mask is not None:
        s = jnp.where(mask, s, NEG)
    m_new = jnp.maximum(m, jnp.max(s, axis=0, keepdims=True))
    acc = jnp.exp2(m - m_new) * acc + _dot(vt, jnp.exp2(s - m_new).astype(BF16))
    return m_new, acc


def _nsa_kernel(q_ref, kv_ref, kvc_ref, zg_ref, ovt_ref, o_ref, kaug_ref, vst_ref, vwt_ref, off_ref, coff_ref):
    seq = kv_ref.shape[0]
    nsel = seq // SLC_BLOCK
    ncmp = kvc_ref.shape[0]
    tq = NSA_TQ
    mcols = B_GROUP * tq
    qi = pl.program_id(2)
    t0 = qi * tq
    upper = lax.broadcasted_iota(jnp.int32, (LANES, KEY_BLK), 0) < HEAD_DIM

    @pl.when(qi == 0)
    def _():
        key = lax.broadcasted_iota(jnp.int32, (seq, LANES), 0)
        lane = lax.broadcasted_iota(jnp.int32, (seq, LANES), 1)
        onehot = jnp.logical_and(lane >= HEAD_DIM, (key // SLC_BLOCK) == lane - HEAD_DIM)
        kaug_ref[...] = jnp.where(lane < HEAD_DIM, kv_ref[:, :LANES],
                                  jnp.where(onehot, 1.0, 0.0).astype(BF16))
        off_ref[...] = (lax.broadcasted_iota(jnp.int32, off_ref.shape, 1) % tq
                        - lax.broadcasted_iota(jnp.int32, off_ref.shape, 0))
        coff_ref[...] = (lax.broadcasted_iota(jnp.int32, coff_ref.shape, 1) % tq
                         - CMP_STRIDE * lax.broadcasted_iota(jnp.int32, coff_ref.shape, 0))
        for kb in range(seq // KEY_BLK):
            rows = slice(kb * KEY_BLK, (kb + 1) * KEY_BLK)
            vst_ref[kb] = jnp.where(upper, 1.0, kv_ref[rows, :LANES].astype(F32).T).astype(BF16)
            vwt_ref[kb] = jnp.where(upper, 1.0, kv_ref[rows, LANES:].astype(F32).T).astype(BF16)

    lane_q = lax.broadcasted_iota(jnp.int32, (tq, LANES), 1)
    low = lane_q < HEAD_DIM
    qf = q_ref[...].astype(F32) * (SCALE * LOG2E)
    parts = []
    for pair in range(2):
        x = qf[:, pair * LANES:(pair + 1) * LANES]
        parts.append(jnp.where(low, x, 0.0))
        parts.append(jnp.where(low, pltpu.roll(x, HEAD_DIM, 1), 0.0))
    q128 = jnp.concatenate(parts, axis=0)
    qw = q128.astype(BF16)

    init = (jnp.full((1, mcols), NEG, F32), jnp.zeros((LANES, mcols), F32))

    kvc = kvc_ref[...]
    s = _dot_nt(kvc.astype(BF16), qw)
    s = jnp.where(coff_ref[...] >= (CMP_LEN - 1) - t0, s, -jnp.inf)
    m = jnp.max(s, axis=0, keepdims=True)
    m = jnp.where(m == -jnp.inf, 0.0, m)
    e = jnp.exp2(s - m)
    l_c = jnp.maximum(jnp.sum(e, axis=0, keepdims=True), 1e-30)
    p = e / l_c
    acc_c = _dot(kvc.T.astype(BF16), p.astype(BF16))

    psum = p[:, 0:tq] + p[:, tq:2 * tq] + p[:, 2 * tq:3 * tq] + p[:, 3 * tq:4 * tq]
    p_hi = psum.astype(BF16)
    r1 = psum - p_hi.astype(F32)
    p_mid = r1.astype(BF16)
    p_lo = (r1 - p_mid.astype(F32)).astype(BF16)
    ovt = ovt_ref[...]
    imp = _dot(ovt, p_hi) + _dot(ovt, p_mid) + _dot(ovt, p_lo)
    blk = lax.broadcasted_iota(jnp.int32, (nsel, tq), 0)
    tcol = t0 + lax.broadcasted_iota(jnp.int32, (nsel, tq), 1)
    cur = tcol // SLC_BLOCK
    forced = jnp.logical_or(blk == 0, jnp.logical_or(blk == cur, blk == cur - 1))
    future = blk * SLC_BLOCK > tcol
    imp = jnp.where(future, -jnp.inf, jnp.where(forced, jnp.inf, imp))
    rank = jnp.zeros((nsel, tq), F32)
    for mp in range(nsel):
        other = imp[mp:mp + 1, :]
        before = jnp.logical_or(other > imp, jnp.logical_and(other == imp, blk > mp))
        rank = rank + jnp.where(before, 1.0, 0.0)
    bias_t = jnp.where(rank < float(min(SLC_TOPK, nsel)), 0.0, NEG)
    bias_t = jnp.concatenate([jnp.zeros((HEAD_DIM, tq), F32), bias_t,
                              jnp.zeros((LANES - HEAD_DIM - nsel, tq), F32)], axis=0)
    bias = bias_t.T
    qs = (q128 + jnp.concatenate([bias] * B_GROUP, axis=0)).astype(BF16)

    def values_t(ref, first_blk, nblk):
        return jnp.concatenate([ref[first_blk + u] for u in range(nblk)], axis=1)

    def lead(nkeys, shift):
        return off_ref[0:nkeys, :] + shift

    not_after = off_ref[0:tq, :] >= 0

    def own_tile(k_own, qop, vt_own):
        s = jnp.where(not_after, _dot_nt(k_own, qop), NEG)
        m_ref = jnp.max(s, axis=0, keepdims=True)
        return m_ref, _dot(vt_own, jnp.exp2(s - m_ref).astype(BF16))

    def weighted_values(s, vt, m_ref, mask):
        if mask is not None:
            s = jnp.where(mask, s, NEG)
        return _dot(vt, jnp.exp2(s - m_ref).astype(BF16))

    blk_per_tile = tq // KEY_BLK
    m_ref_w, acc_w = own_tile(kv_ref[pl.ds(pl.multiple_of(t0, tq), tq), LANES:], qw,
                              values_t(vwt_ref, qi * blk_per_tile, blk_per_tile))
    for back in range(1, WIN // tq + 1):
        first = t0 - back * tq
        start = jnp.maximum(first, 0)
        shift = jnp.where(first >= 0, back * tq, 2 * WIN + tq)
        inside = lax.bitcast_convert_type(lead(tq, shift), jnp.uint32) < jnp.uint32(WIN)
        acc_w = acc_w + weighted_values(_dot_nt(kv_ref[pl.ds(pl.multiple_of(start, tq), tq), LANES:], qw),
                                        values_t(vwt_ref, start // KEY_BLK, blk_per_tile), m_ref_w, inside)

    def slc_tiles(own):
        def run():
            k0 = own * tq
            m_ref, acc = own_tile(kaug_ref[k0:k0 + tq, :], qs, values_t(vst_ref, own * blk_per_tile, blk_per_tile))
            for first, nkeys in [(j * NSA_TK, NSA_TK) for j in range(k0 // NSA_TK)] + (
                    [(k0 - tq, tq)] if k0 % NSA_TK else []):
                acc = acc + weighted_values(_dot_nt(kaug_ref[first:first + nkeys, :], qs),
                                            values_t(vst_ref, first // KEY_BLK, nkeys // KEY_BLK), m_ref, None)
            return acc
        return run

    acc_s = lax.switch(qi, [slc_tiles(n) for n in range(seq // tq)])

    gate_t = jax.nn.sigmoid(zg_ref[...]).T

    def grow(br):
        return jnp.concatenate([gate_t[3 * g + br:3 * g + br + 1, :] for g in range(B_GROUP)], axis=1)

    def gated_sum(a_s, a_w):
        return acc_c * grow(0) + a_s * (grow(1) / a_s[0:1, :]) + a_w * (grow(2) / a_w[0:1, :])

    def running_max_path():
        w0 = jnp.maximum(t0 - WIN, 0)
        wrows = pl.ds(pl.multiple_of(w0, tq), WIN + tq)
        in_window = lax.bitcast_convert_type(lead(WIN + tq, t0 - w0), jnp.uint32) < jnp.uint32(WIN)
        _, a_w = _softmax_step_t(_dot_nt(kv_ref[wrows, LANES:], qw),
                                 values_t(vwt_ref, w0 // KEY_BLK, (WIN + tq) // KEY_BLK), *init, mask=in_window)
        nblk = NSA_TK // KEY_BLK

        def body(j, carry):
            rows = pl.ds(pl.multiple_of(j * NSA_TK, NSA_TK), NSA_TK)
            return _softmax_step_t(_dot_nt(kaug_ref[rows, :], qs), values_t(vst_ref, j * nblk, nblk),
                                   *carry, mask=lead(NSA_TK, t0 - j * NSA_TK) >= 0)

        _, a_s = lax.fori_loop(0, t0 // NSA_TK + 1, body, init)
        return gated_sum(a_s, a_w)

    y_t = gated_sum(acc_s, acc_w)
    finite = jnp.logical_and(jnp.isfinite(y_t[HEAD_DIM:, :]),
                             jnp.isfinite(acc_s[0:1, :] + acc_w[0:1, :]))
    overflowed = jnp.max(jnp.where(finite, 0.0, 1.0)) > 0.0
    y_t = lax.cond(overflowed, running_max_path, lambda: y_t)
    for pair in range(2):
        even = y_t[:, (2 * pair) * tq:(2 * pair + 1) * tq].T
        odd = y_t[:, (2 * pair + 1) * tq:(2 * pair + 2) * tq].T
        o_ref[:, pair * LANES:(pair + 1) * LANES] = jnp.where(
            low, pltpu.roll(even, HEAD_DIM, 1), odd).astype(o_ref.dtype)


def _nsa(zq, zkv, kvc, zg, ovt):
    b, seq, _ = zq.shape
    ncmp = kvc.shape[2]
    nsel = seq // SLC_BLOCK
    return pl.pallas_call(
        _nsa_kernel,
        grid=(b, B_KV_HEADS, seq // NSA_TQ),
        in_specs=[pl.BlockSpec((None, NSA_TQ, MXU_DIM), lambda i, h, q: (i, q, h)),
                  pl.BlockSpec((None, seq, MXU_DIM), lambda i, h, q: (i, 0, h)),
                  pl.BlockSpec((None, None, ncmp, LANES), lambda i, h, q: (i, h, 0, 0)),
                  pl.BlockSpec((None, NSA_TQ, LANES), lambda i, h, q: (i, q, h)),
                  pl.BlockSpec((nsel, ncmp), lambda i, h, q: (0, 0))],
        out_specs=pl.BlockSpec((None, NSA_TQ, MXU_DIM), lambda i, h, q: (i, q, h)),
        out_shape=jax.ShapeDtypeStruct((b, seq, B_Q), BF16),
        scratch_shapes=[pltpu.VMEM((seq, LANES), BF16)]
        + [pltpu.VMEM((seq // KEY_BLK, LANES, KEY_BLK), BF16)] * 2
        + [pltpu.VMEM((WIN + NSA_TQ, B_GROUP * NSA_TQ), jnp.int32),
           pltpu.VMEM((ncmp, B_GROUP * NSA_TQ), jnp.int32)],
        compiler_params=_params("parallel", "parallel", "arbitrary"),
        name="nsa",
    )(zq, zkv, kvc, zg, ovt)


def _merge_kernel(h_ref, ya_ref, yb_ref, wg_ref, wba_ref, wbb_ref, wmix_ref, g_ref, b_ref, o_ref):
    h = h_ref[...]
    hb = h.astype(BF16)
    ya = ya_ref[...]
    yb = yb_ref[...]
    nchunk = D_MODEL // MXU_DIM

    def pre(n):
        cols = slice(n * MXU_DIM, (n + 1) * MXU_DIM)
        gcols = slice(D_MODEL + n * MXU_DIM, D_MODEL + (n + 1) * MXU_DIM)
        return (_dot(hb, wg_ref[:, cols]), _dot(ya, wba_ref[:, cols]),
                _dot(hb, wg_ref[:, gcols]), _dot(yb, wbb_ref[:, cols]))

    ahead = [pre(n) for n in range(2)]
    mix = jnp.zeros(h.shape, F32)
    for n in range(nchunk):
        za, pa, zb, pb = ahead.pop(0)
        if n + 2 < nchunk:
            ahead.append(pre(n + 2))
        merged = jax.nn.sigmoid(za) * pa + jax.nn.sigmoid(zb) * pb
        mix = mix + _dot(merged.astype(BF16), wmix_ref[n * MXU_DIM:(n + 1) * MXU_DIM, :])
    o_ref[...] = _layer_norm(ALPHA * h + mix, g_ref[...], b_ref[...])


def _merge(h2d, ya, yb, wg, wba, wbb, wmix, g, b, tm):
    t = h2d.shape[0]
    row = lambda i: (i, 0)
    const = lambda i: (0, 0)
    return pl.pallas_call(
        _merge_kernel,
        grid=(t // tm,),
        in_specs=[pl.BlockSpec((tm, D_MODEL), row), pl.BlockSpec((tm, A_OUT), row),
                  pl.BlockSpec((tm, B_Q), row), pl.BlockSpec((D_MODEL, 2 * D_MODEL), const),
                  pl.BlockSpec((A_OUT, D_MODEL), const), pl.BlockSpec((B_Q, D_MODEL), const),
                  pl.BlockSpec((D_MODEL, D_MODEL), const), pl.BlockSpec((1, D_MODEL), const),
                  pl.BlockSpec((1, D_MODEL), const)],
        out_specs=pl.BlockSpec((tm, D_MODEL), row),
        out_shape=jax.ShapeDtypeStruct((t, D_MODEL), F32),
        compiler_params=_params("parallel"),
        name="merge",
    )(h2d, ya, yb, wg, wba, wbb, wmix, g, b)


def _xkv_kernel(mem_ref, w_ref, o_ref):
    o_ref[...] = _dot(mem_ref[...].astype(BF16), w_ref[...]).astype(o_ref.dtype)


def _xkv(mem, w):
    b, mlen, _ = mem.shape
    return pl.pallas_call(
        _xkv_kernel,
        grid=(b,),
        in_specs=[pl.BlockSpec((None, mlen, D_MODEL), lambda i: (i, 0, 0)),
                  pl.BlockSpec((D_MODEL, 2 * X_DIM), lambda i: (0, 0))],
        out_specs=pl.BlockSpec((None, mlen, 2 * X_DIM), lambda i: (i, 0, 0)),
        out_shape=jax.ShapeDtypeStruct((b, mlen, 2 * X_DIM), BF16),
        compiler_params=_params("parallel"),
        name="xkv",
    )(mem, w)


def _xattn_kernel(h_ref, kv_ref, wq_ref, wo_ref, g_ref, b_ref, o_ref):
    h = h_ref[...]
    tm = h.shape[0]
    q = _dot(h.astype(BF16), wq_ref[...]) * SCALE
    k = kv_ref[:, :X_DIM]
    v = kv_ref[:, X_DIM:]
    lane = lax.broadcasted_iota(jnp.int32, (tm, X_DIM), 1)
    o = jnp.zeros((tm, X_DIM), F32)
    for hd in range(X_HEADS):
        in_head = (lane // HEAD_DIM) == hd
        s = _dot_nt(jnp.where(in_head, q, 0.0).astype(BF16), k)
        m = jnp.max(s, axis=-1, keepdims=True)
        e = jnp.exp(s - m)
        p = e / jnp.sum(e, axis=-1, keepdims=True)
        o = jnp.where(in_head, _dot(p.astype(BF16), v), o)
    att = _dot(o.astype(BF16), wo_ref[...])
    o_ref[...] = _layer_norm(ALPHA * h + att, g_ref[...], b_ref[...])


def _xattn(h2d, kv, wq, wo, g, b, seq, tm):
    t = h2d.shape[0]
    mlen = kv.shape[1]
    nseq = seq // tm
    row = lambda i: (i, 0)
    const = lambda i: (0, 0)
    return pl.pallas_call(
        _xattn_kernel,
        grid=(t // tm,),
        in_specs=[pl.BlockSpec((tm, D_MODEL), row),
                  pl.BlockSpec((None, mlen, 2 * X_DIM), lambda i: (i // nseq, 0, 0)),
                  pl.BlockSpec((D_MODEL, X_DIM), const), pl.BlockSpec((X_DIM, D_MODEL), const),
                  pl.BlockSpec((1, D_MODEL), const), pl.BlockSpec((1, D_MODEL), const)],
        out_specs=pl.BlockSpec((tm, D_MODEL), row),
        out_shape=jax.ShapeDtypeStruct((t, D_MODEL), F32),
        compiler_params=_params("parallel"),
        name="xattn",
    )(h2d, kv, wq, wo, g, b)


FFN_CHUNK = 256
FFN_AHEAD = 2


def _ffn_kernel(h_ref, wup_ref, cw_ref, cb_ref, wdn_ref, g_ref, b_ref, o_ref, tail_sc, *, tiles_per_seq):
    tm = h_ref.shape[0]
    i = pl.program_id(0)

    @pl.when(i % tiles_per_seq == 0)
    def _():
        tail_sc[...] = jnp.zeros(tail_sc.shape, F32)

    h = h_ref[...]
    hb = h.astype(BF16)
    acc = jnp.zeros((tm, D_MODEL), F32)
    nchunk = D_FF // FFN_CHUNK
    top = lax.broadcasted_iota(jnp.int32, (SUBLANES, FFN_CHUNK), 0)

    def up(c):
        return [_dot(hb, wup_ref[:, base:base + FFN_CHUNK]) for base in (c * FFN_CHUNK, D_FF + c * FFN_CHUNK)]

    def delayed(u, tail, d):
        ur = pltpu.roll(u, d, 0)
        first = jnp.where(top < d, pltpu.roll(tail, d, 0), ur[:SUBLANES])
        return jnp.concatenate([first, ur[SUBLANES:]], axis=0)

    ahead = [up(c) for c in range(FFN_AHEAD)]
    for c in range(nchunk):
        u_cur = ahead.pop(0)
        if c + FFN_AHEAD < nchunk:
            ahead.append(up(c + FFN_AHEAD))
        halves = []
        for u, base in zip(u_cur, (c * FFN_CHUNK, D_FF + c * FFN_CHUNK)):
            cols = slice(base, base + FFN_CHUNK)
            tail = tail_sc[:, cols]
            tail_sc[:, cols] = u[tm - SUBLANES:]
            halves.append(cw_ref[2:3, cols] * u + cw_ref[1:2, cols] * delayed(u, tail, 1)
                          + cw_ref[0:1, cols] * delayed(u, tail, 2) + cb_ref[:, cols])
        act = _gelu_tanh(halves[0]) * halves[1]
        acc = acc + _dot(act.astype(BF16), wdn_ref[c * FFN_CHUNK:(c + 1) * FFN_CHUNK, :])
    o_ref[...] = _layer_norm(ALPHA * h + acc, g_ref[...], b_ref[...])


def _ffn(h2d, wup, cw, cb, wdn, g, b, seq, tm):
    t = h2d.shape[0]
    row = lambda i: (i, 0)
    const = lambda i: (0, 0)
    return pl.pallas_call(
        functools.partial(_ffn_kernel, tiles_per_seq=seq // tm),
        grid=(t // tm,),
        in_specs=[pl.BlockSpec((tm, D_MODEL), row),
                  pl.BlockSpec((D_MODEL, 2 * D_FF), const, pipeline_mode=pl.Buffered(1)),
                  pl.BlockSpec((CONV_W, 2 * D_FF), const), pl.BlockSpec((1, 2 * D_FF), const),
                  pl.BlockSpec((D_FF, D_MODEL), const, pipeline_mode=pl.Buffered(1)),
                  pl.BlockSpec((1, D_MODEL), const), pl.BlockSpec((1, D_MODEL), const)],
        out_specs=pl.BlockSpec((tm, D_MODEL), row),
        out_shape=jax.ShapeDtypeStruct((t, D_MODEL), F32),
        scratch_shapes=[pltpu.VMEM((SUBLANES, 2 * D_FF), F32)],
        compiler_params=_params("arbitrary"),
        name="ffn",
    )(h2d, wup, cw, cb, wdn, g, b)


def _rope_tables(seq):
    inv = ROPE_THETA ** (-jnp.arange(HALF, dtype=F32) / HALF)
    ang = jnp.arange(seq, dtype=F32)[:, None] * inv[None, :]
    cos, sin = jnp.cos(ang), jnp.sin(ang)
    zero, one = jnp.zeros_like(cos), jnp.ones_like(cos)
    cq = jnp.concatenate([cos, cos, cos, cos], axis=1)
    s1q = jnp.concatenate([-sin, zero, -sin, zero], axis=1)
    s2q = jnp.concatenate([zero, sin, zero, sin], axis=1)
    ck = jnp.concatenate([cos, cos, one, one], axis=1)
    s1k = jnp.concatenate([-sin, zero, zero, zero], axis=1)
    s2k = jnp.concatenate([zero, sin, zero, zero], axis=1)
    pos_c = (CMP_STRIDE * jnp.arange(seq // CMP_STRIDE) + CMP_LEN - 1).astype(F32)
    ang_c = pos_c[:, None] * inv[None, :]
    return (cq, s1q, s2q, ck, s1k, s2k), jnp.cos(ang_c), jnp.sin(ang_c)


def _overlap_t(seq):
    ncmp = seq // CMP_STRIDE
    nsel = seq // SLC_BLOCK
    c_start = CMP_STRIDE * jnp.arange(ncmp)
    s_start = SLC_BLOCK * jnp.arange(nsel)
    ov = jnp.clip(jnp.minimum(c_start[None, :] + CMP_LEN, s_start[:, None] + SLC_BLOCK)
                  - jnp.maximum(c_start[None, :], s_start[:, None]), 0).astype(F32) / CMP_LEN
    return ov.astype(BF16)


def _pack_w_in(w):
    o1 = A_QKV
    o2 = o1 + B_Q
    o3 = o2 + B_KV
    o4 = o3 + B_GATE
    wa, wq, wkv, wbg, wgate = w[:, :o1], w[:, o1:o2], w[:, o2:o3], w[:, o3:o4], w[:, o4:]
    kv = wkv.reshape(D_MODEL, 3, 2, B_KV_HEADS, HEAD_DIM)
    per_head = [jnp.concatenate([kv[:, 1, 0, hh], kv[:, 1, 1, hh], kv[:, 2, 0, hh], kv[:, 2, 1, hh]], axis=1)
                for hh in range(B_KV_HEADS)]
    cmp_blk = wkv[:, :2 * B_KV_HEADS * HEAD_DIM]
    per_gate = B_GROUP * 3
    gates = [jnp.pad(wbg[:, hh * per_gate:(hh + 1) * per_gate], ((0, 0), (0, LANES - per_gate)))
             for hh in range(B_KV_HEADS)]
    wp = jnp.concatenate([wa, wq] + per_head + [cmp_blk] + gates, axis=1)
    return wp.astype(BF16), wgate.astype(BF16)


def kernel(x, mem, w_in, cmp_pos, cmp_w1, cmp_b1, cmp_w2, w_branch_a, w_branch_b, w_mix_out, ln1_g, ln1_b,
           w_xq, w_xkv, w_xo, ln2_g, ln2_b, w_up, conv_w, conv_b, w_down, ln3_g, ln3_b):
    b, seq, _ = x.shape
    t = b * seq
    tabs, cos_c, sin_c = _rope_tables(seq)
    ovt = _overlap_t(seq)
    h = x.reshape(t, D_MODEL)
    for l in range(DEPTH):
        wp, wgate = _pack_w_in(w_in[l])
        za, zq, zkv, zc, zg = _project(h, wp, tabs, seq, tm=512)
        kvc = _compress(zc.reshape(b, seq, MXU_DIM),
                        cmp_w1[l].reshape(2, CMP_LEN, HEAD_DIM, CMP_HIDDEN).astype(BF16),
                        cmp_pos[l], cmp_b1[l], cmp_w2[l].astype(BF16), cos_c, sin_c)
        ya = _attn_a(za.reshape(b, seq, A_QKV))
        yb = _nsa(zq.reshape(b, seq, B_Q), zkv.reshape(b, seq, 2 * MXU_DIM), kvc,
                  zg.reshape(b, seq, MXU_DIM), ovt)
        h = _merge(h, ya.reshape(t, A_OUT), yb.reshape(t, B_Q), wgate, w_branch_a[l].astype(BF16),
                   w_branch_b[l].astype(BF16), w_mix_out[l].astype(BF16),
                   ln1_g[l][None, :], ln1_b[l][None, :], tm=512)
        xkv = _xkv(mem, w_xkv[l].astype(BF16))
        h = _xattn(h, xkv, w_xq[l].astype(BF16), w_xo[l].astype(BF16),
                   ln2_g[l][None, :], ln2_b[l][None, :], seq, tm=512)
        h = _ffn(h, w_up[l].astype(BF16), conv_w[l], conv_b[l][None, :], w_down[l].astype(BF16),
                 ln3_g[l][None, :], ln3_b[l][None, :], seq, tm=256)
    return h.reshape(b, seq, D_MODEL)
```

```python
import functools

import jax
import jax.numpy as jnp
from jax import lax
from jax.experimental import pallas as pl
from jax.experimental.pallas import tpu as pltpu

F32 = jnp.float32
BF16 = jnp.bfloat16

D_MODEL = 1024
DEPTH = 2
HEAD_DIM = 64
HALF = HEAD_DIM // 2
ROPE_THETA = 10000.0
LN_EPS = 1e-5
ALPHA = (2 * DEPTH) ** 0.25
SCALE = HEAD_DIM ** -0.5

DIL_PATTERNS = ((128, 1), (512, 4), (2048, 16))
A_GROUPS = 3
A_HEADS_PER_GROUP = 4
A_QKV = 3 * A_GROUPS * A_HEADS_PER_GROUP * HEAD_DIM
A_OUT = A_HEADS_PER_GROUP * HEAD_DIM

B_Q_HEADS = 8
B_KV_HEADS = 2
B_GROUP = 4
B_Q = B_Q_HEADS * HEAD_DIM
B_KV = 3 * 2 * B_KV_HEADS * HEAD_DIM
B_GATE = 3 * B_Q_HEADS
CMP_STRIDE = 16
CMP_LEN = 32
CMP_HIDDEN = 256
SLC_BLOCK = 64
SLC_TOPK = 8
WIN = 512

X_HEADS = 4
X_DIM = X_HEADS * HEAD_DIM
D_FF = 2816
CONV_W = 3

LANES = 128
SUBLANES = 8
MXU_DIM = 256
VMEM_LIMIT_BYTES = 56 * 1024 * 1024

NEG = -1e30
LOG2E = 1.4426950408889634

N_PROJ_BLOCKS = 15
PROJ_COLS = N_PROJ_BLOCKS * MXU_DIM
PROJ_AHEAD = 2

_NT = (((1,), (1,)), ((), ()))


def _dot(a, b):
    return jnp.dot(a, b, preferred_element_type=F32)


def _dot_nt(a, b):
    return lax.dot_general(a, b, _NT, preferred_element_type=F32)


def _params(*sem):
    return pltpu.CompilerParams(dimension_semantics=sem, vmem_limit_bytes=VMEM_LIMIT_BYTES)


def _layer_norm(x, g, b):
    mu = jnp.mean(x, axis=-1, keepdims=True)
    xc = x - mu
    var = jnp.mean(xc * xc, axis=-1, keepdims=True)
    return xc * lax.rsqrt(var + LN_EPS) * g + b


def _gelu_tanh(x):
    c = 0.7978845608028654
    half = 0.5 * x
    return half + half * jnp.tanh(x * (c + (c * 0.044715) * (x * x)))


def _rope128(z, c, s1, s2):
    return z * c + pltpu.roll(z, LANES - HALF, 1) * s1 + pltpu.roll(z, HALF, 1) * s2


def _proj_kernel(x_ref, w_ref, cq_ref, s1q_ref, s2q_ref, ck_ref, s1k_ref, s2k_ref,
                 za_ref, zq_ref, zkv_ref, zc_ref, zg_ref):
    xb = x_ref[...].astype(BF16)

    def block(j):
        return _dot(xb, w_ref[:, j * MXU_DIM:(j + 1) * MXU_DIM])

    ahead = [block(j) for j in range(PROJ_AHEAD)]
    for j in range(N_PROJ_BLOCKS):
        z = ahead.pop(0)
        if j + PROJ_AHEAD < N_PROJ_BLOCKS:
            ahead.append(block(j + PROJ_AHEAD))
        if j < 6 or 9 <= j <= 10:
            tabs = (cq_ref[...], s1q_ref[...], s2q_ref[...])
        elif 11 <= j <= 12:
            tabs = (ck_ref[...], s1k_ref[...], s2k_ref[...])
        else:
            tabs = None
        if tabs is not None:
            z = jnp.concatenate([_rope128(z[:, :LANES], *tabs), _rope128(z[:, LANES:], *tabs)], axis=1)
        if j < 9:
            za_ref[:, j * MXU_DIM:(j + 1) * MXU_DIM] = z
        elif j < 11:
            zq_ref[:, (j - 9) * MXU_DIM:(j - 8) * MXU_DIM] = z.astype(BF16)
        elif j < 13:
            zkv_ref[:, (j - 11) * MXU_DIM:(j - 10) * MXU_DIM] = z.astype(BF16)
        elif j == 13:
            zc_ref[...] = z
        else:
            zg_ref[...] = z


def _project(h2d, w, tabs, seq, tm):
    t = h2d.shape[0]
    nseq = seq // tm
    row = lambda i: (i, 0)
    tab_spec = pl.BlockSpec((tm, LANES), lambda i: (i % nseq, 0))
    return pl.pallas_call(
        _proj_kernel,
        grid=(t // tm,),
        in_specs=[pl.BlockSpec((tm, D_MODEL), row),
                  pl.BlockSpec((D_MODEL, PROJ_COLS), lambda i: (0, 0))] + [tab_spec] * 6,
        out_specs=[pl.BlockSpec((tm, A_QKV), row), pl.BlockSpec((tm, B_Q), row),
                   pl.BlockSpec((tm, 2 * MXU_DIM), row), pl.BlockSpec((tm, MXU_DIM), row),
                   pl.BlockSpec((tm, MXU_DIM), row)],
        out_shape=[jax.ShapeDtypeStruct((t, A_QKV), F32), jax.ShapeDtypeStruct((t, B_Q), BF16),
                   jax.ShapeDtypeStruct((t, 2 * MXU_DIM), BF16), jax.ShapeDtypeStruct((t, MXU_DIM), F32),
                   jax.ShapeDtypeStruct((t, MXU_DIM), F32)],
        compiler_params=_params("parallel"),
        name="proj",
    )(h2d, w, *tabs)


def _cmp_kernel(zc_ref, w1_ref, pe_ref, b1_ref, w2_ref, cos_ref, sin_ref, o_ref, xk_ref, xv_ref):
    seq = zc_ref.shape[0]
    nblk = seq // CMP_STRIDE
    xs_refs = (xk_ref, xv_ref)
    for kv in range(2):
        xs_refs[kv][0:seq, :] = zc_ref[:, kv * LANES:(kv + 1) * LANES]
        xs_refs[kv][seq:seq + CMP_LEN, :] = jnp.zeros((CMP_LEN, LANES), F32)
    acc = [jnp.zeros((nblk, CMP_HIDDEN), F32) for _ in range(4)]
    for j in range(CMP_LEN):
        xj = [r[pl.ds(j, nblk, stride=CMP_STRIDE), :] for r in xs_refs]
        for c in range(4):
            kv, hh = c // 2, c % 2
            piece = xj[kv][:, hh * HEAD_DIM:(hh + 1) * HEAD_DIM] + pe_ref[kv, j:j + 1, :]
            acc[c] = acc[c] + _dot(piece.astype(BF16), w1_ref[kv, j])
    outs = []
    for c in range(4):
        kv = c // 2
        hid = jax.nn.gelu(acc[c] + b1_ref[kv:kv + 1, :])
        y = _dot(hid.astype(BF16), w2_ref[kv])
        if kv == 0:
            y1, y2 = y[:, :HALF], y[:, HALF:]
            cs, sn = cos_ref[...], sin_ref[...]
            y = jnp.concatenate([y1 * cs - y2 * sn, y1 * sn + y2 * cs], axis=1)
        outs.append(y)
    o_ref[0] = jnp.concatenate([outs[0], outs[2]], axis=1)
    o_ref[1] = jnp.concatenate([outs[1], outs[3]], axis=1)


def _compress(zc, w1, pe, b1, w2, cos_c, sin_c):
    b, seq, _ = zc.shape
    nblk = seq // CMP_STRIDE
    full = lambda *shape: pl.BlockSpec(shape, lambda i: (0,) * len(shape))
    return pl.pallas_call(
        _cmp_kernel,
        grid=(b,),
        in_specs=[pl.BlockSpec((None, seq, MXU_DIM), lambda i: (i, 0, 0)),
                  full(2, CMP_LEN, HEAD_DIM, CMP_HIDDEN), full(2, CMP_LEN, HEAD_DIM),
                  full(2, CMP_HIDDEN), full(2, CMP_HIDDEN, HEAD_DIM),
                  full(nblk, HALF), full(nblk, HALF)],
        out_specs=pl.BlockSpec((None, 2, nblk, LANES), lambda i: (i, 0, 0, 0)),
        out_shape=jax.ShapeDtypeStruct((b, 2, nblk, LANES), F32),
        scratch_shapes=[pltpu.VMEM((seq + CMP_LEN, LANES), F32), pltpu.VMEM((seq + CMP_LEN, LANES), F32)],
        compiler_params=_params("parallel"),
        name="compress",
    )(zc, w1, pe, b1, w2, cos_c, sin_c)


A_TILE = 128
A_UNROLL = 8
FREE_STRIDE = 4


def _attn_a_kernel(q0, k0, v0, q1, k1, v1, q2, k2, v2, y_ref, o_sc, l_sc, tq_sc, tk_sc, tv_sc, to_sc, tl_sc):
    seq = y_ref.shape[0]
    qkv = ((q0, k0, v0), (q1, k1, v1), (q2, k2, v2))
    first_head = lax.broadcasted_iota(jnp.int32, (A_TILE, LANES), 1) < HEAD_DIM

    for g, (_, dil) in enumerate(DIL_PATTERNS):
        q_ref, k_ref, v_ref = qkv[g]
        o_dst, l_dst = o_sc.at[g], l_sc.at[g]
        ntile = seq // dil // A_TILE
        span = A_TILE * dil
        nkeys = 2 * A_TILE if ntile > 1 else A_TILE
        row = lax.broadcasted_iota(jnp.int32, (A_TILE, nkeys), 0)
        col = lax.broadcasted_iota(jnp.int32, (A_TILE, nkeys), 1)
        band = jnp.logical_and(col >= row, col <= row + A_TILE) if ntile > 1 else col <= row
        own = col >= A_TILE
        two_stage = dil > FREE_STRIDE
        if two_stage:
            assert ntile == 1 and dil % FREE_STRIDE == 0
            part = seq // FREE_STRIDE
            srcs = (q_ref, k_ref, v_ref)
            q_ref, k_ref, v_ref, o_dst, l_dst = tq_sc, tk_sc, tv_sc, to_sc, tl_sc
            for src, dst in zip(srcs, (q_ref, k_ref, v_ref)):
                for r4 in range(FREE_STRIDE):
                    dst[r4 * part:(r4 + 1) * part, :] = src[pl.ds(r4, part, stride=FREE_STRIDE), :]

        def rows_at(start, dil=dil, two_stage=two_stage):
            if dil == 1:
                return pl.ds(pl.multiple_of(start, A_TILE), A_TILE)
            if two_stage:
                return pl.ds((start % FREE_STRIDE) * (seq // FREE_STRIDE) + start // FREE_STRIDE, A_TILE,
                             stride=dil // FREE_STRIDE)
            return pl.ds(start, A_TILE, stride=dil)

        def load(n, q_ref=q_ref, k_ref=k_ref, v_ref=v_ref, ntile=ntile, span=span, rows_at=rows_at,
                 band=band, own=own):
            r = n // ntile
            i = n % ntile
            start = r + i * span
            rows = rows_at(start)
            q = q_ref[rows, :] * (SCALE * LOG2E)
            q2 = jnp.concatenate([jnp.where(first_head, q, 0.0), jnp.where(first_head, 0.0, q)], axis=0).astype(BF16)
            k = k_ref[rows, :].astype(BF16)
            v = v_ref[rows, :].astype(BF16)
            valid = band
            if ntile > 1:
                prows = rows_at(jnp.maximum(start - span, r))
                k = jnp.concatenate([k_ref[prows, :].astype(BF16), k], axis=0)
                v = jnp.concatenate([v_ref[prows, :].astype(BF16), v], axis=0)
                valid = jnp.logical_and(band, jnp.logical_or(own, i > 0))
            return rows, q2, k, v, jnp.concatenate([valid, valid], axis=0)

        def body(it, carry, load=load, o_dst=o_dst, l_dst=l_dst):
            tiles = [load(it * A_UNROLL + u) for u in range(A_UNROLL)]
            scores = [jnp.where(valid, _dot_nt(q2, k), NEG) for _, q2, k, _, valid in tiles]
            for (rows, _, _, v, _), s in zip(tiles, scores):
                m = jnp.max(s, axis=-1, keepdims=True)
                e = jnp.exp2(s - m)
                l = jnp.sum(e, axis=-1, keepdims=True)
                o = _dot(e.astype(BF16), v) / l
                lse = jnp.broadcast_to(jnp.log2(l) + m, o.shape)
                o_dst[rows, :] = jnp.where(first_head, o[:A_TILE], o[A_TILE:])
                l_dst[rows, :] = jnp.where(first_head, lse[:A_TILE], lse[A_TILE:])
            return carry

        lax.fori_loop(0, seq // A_TILE // A_UNROLL, body, 0)
        if two_stage:
            for src, dst in ((o_dst, o_sc.at[g]), (l_dst, l_sc.at[g])):
                for r4 in range(FREE_STRIDE):
                    dst[pl.ds(r4, part, stride=FREE_STRIDE), :] = src[r4 * part:(r4 + 1) * part, :]

    chunk = 256
    for c in range(seq // chunk):
        rows = pl.ds(c * chunk, chunk)
        ls = [l_sc[g, rows, :] for g in range(A_GROUPS)]
        mx = jnp.maximum(jnp.maximum(ls[0], ls[1]), ls[2])
        ws = [jnp.exp2(l - mx) for l in ls]
        num = ws[0] * o_sc[0, rows, :] + ws[1] * o_sc[1, rows, :] + ws[2] * o_sc[2, rows, :]
        y_ref[rows, :] = (num / (ws[0] + ws[1] + ws[2])).astype(y_ref.dtype)


def _attn_a(za):
    b, seq, _ = za.shape
    in_specs = []
    for g in range(A_GROUPS):
        for part in range(3):
            blk = part * 2 * A_GROUPS + 2 * g
            in_specs.append(pl.BlockSpec((None, seq, LANES), lambda i, hp, blk=blk: (i, 0, blk + hp)))
    return pl.pallas_call(
        _attn_a_kernel,
        grid=(b, 2),
        in_specs=in_specs,
        out_specs=pl.BlockSpec((None, seq, LANES), lambda i, hp: (i, 0, hp)),
        out_shape=jax.ShapeDtypeStruct((b, seq, A_OUT), BF16),
        scratch_shapes=[pltpu.VMEM((A_GROUPS, seq, LANES), F32), pltpu.VMEM((A_GROUPS, seq, LANES), F32),
                        ] + [pltpu.VMEM((seq, LANES), F32)] * 5,
        compiler_params=_params("parallel", "parallel"),
        name="attn_a",
    )(*([za] * 9))


NSA_TQ = 256
NSA_TK = 512
KEY_BLK = 128


def _softmax_step_t(s, vt, m, acc, mask=None):
    if mask is not None:
        s = jnp.where(mask, s, NEG)
    m_new = jnp.maximum(m, jnp.max(s, axis=0, keepdims=True))
    acc = jnp.exp2(m - m_new) * acc + _dot(vt, jnp.exp2(s - m_new).astype(BF16))
    return m_new, acc


def _nsa_kernel(q_ref, kv_ref, kvc_ref, zg_ref, ovt_ref, o_ref, kaug_ref, vst_ref, vwt_ref, off_ref, coff_ref):
    seq = kv_ref.shape[0]
    nsel = seq // SLC_BLOCK
    ncmp = kvc_ref.shape[0]
    tq = NSA_TQ
    mcols = B_GROUP * tq
    qi = pl.program_id(2)
    t0 = qi * tq
    upper = lax.broadcasted_iota(jnp.int32, (LANES, KEY_BLK), 0) < HEAD_DIM

    @pl.when(qi == 0)
    def _():
        key = lax.broadcasted_iota(jnp.int32, (seq, LANES), 0)
        lane = lax.broadcasted_iota(jnp.int32, (seq, LANES), 1)
        onehot = jnp.logical_and(lane >= HEAD_DIM, (key // SLC_BLOCK) == lane - HEAD_DIM)
        kaug_ref[...] = jnp.where(lane < HEAD_DIM, kv_ref[:, :LANES],
                                  jnp.where(onehot, 1.0, 0.0).astype(BF16))
        off_ref[...] = (lax.broadcasted_iota(jnp.int32, off_ref.shape, 1) % tq
                        - lax.broadcasted_iota(jnp.int32, off_ref.shape, 0))
        coff_ref[...] = (lax.broadcasted_iota(jnp.int32, coff_ref.shape, 1) % tq
                         - CMP_STRIDE * lax.broadcasted_iota(jnp.int32, coff_ref.shape, 0))
        for kb in range(seq // KEY_BLK):
            rows = slice(kb * KEY_BLK, (kb + 1) * KEY_BLK)
            vst_ref[kb] = jnp.where(upper, 1.0, kv_ref[rows, :LANES].astype(F32).T).astype(BF16)
            vwt_ref[kb] = jnp.where(upper, 1.0, kv_ref[rows, LANES:].astype(F32).T).astype(BF16)

    lane_q = lax.broadcasted_iota(jnp.int32, (tq, LANES), 1)
    low = lane_q < HEAD_DIM
    qf = q_ref[...].astype(F32) * (SCALE * LOG2E)
    parts = []
    for pair in range(2):
        x = qf[:, pair * LANES:(pair + 1) * LANES]
        parts.append(jnp.where(low, x, 0.0))
        parts.append(jnp.where(low, pltpu.roll(x, HEAD_DIM, 1), 0.0))
    q128 = jnp.concatenate(parts, axis=0)
    qw = q128.astype(BF16)

    init = (jnp.full((1, mcols), NEG, F32), jnp.zeros((LANES, mcols), F32))

    kvc = kvc_ref[...]
    s = _dot_nt(kvc.astype(BF16), qw)
    s = jnp.where(coff_ref[...] >= (CMP_LEN - 1) - t0, s, -jnp.inf)
    m = jnp.max(s, axis=0, keepdims=True)
    m = jnp.where(m == -jnp.inf, 0.0, m)
    e = jnp.exp2(s - m)
    l_c = jnp.maximum(jnp.sum(e, axis=0, keepdims=True), 1e-30)
    p = e / l_c
    acc_c = _dot(kvc.T.astype(BF16), p.astype(BF16))

    psum = p[:, 0:tq] + p[:, tq:2 * tq] + p[:, 2 * tq:3 * tq] + p[:, 3 * tq:4 * tq]
    p_hi = psum.astype(BF16)
    r1 = psum - p_hi.astype(F32)
    p_mid = r1.astype(BF16)
    p_lo = (r1 - p_mid.astype(F32)).astype(BF16)
    ovt = ovt_ref[...]
    imp = _dot(ovt, p_hi) + _dot(ovt, p_mid) + _dot(ovt, p_lo)
    blk = lax.broadcasted_iota(jnp.int32, (nsel, tq), 0)
    tcol = t0 + lax.broadcasted_iota(jnp.int32, (nsel, tq), 1)
    cur = tcol // SLC_BLOCK
    forced = jnp.logical_or(blk == 0, jnp.logical_or(blk == cur, blk == cur - 1))
    future = blk * SLC_BLOCK > tcol
    imp = jnp.where(future, -jnp.inf, jnp.where(forced, jnp.inf, imp))
    rank = jnp.zeros((nsel, tq), F32)
    for mp in range(nsel):
        other = imp[mp:mp + 1, :]
        before = jnp.logical_or(other > imp, jnp.logical_and(other == imp, blk > mp))
        rank = rank + jnp.where(before, 1.0, 0.0)
    bias_t = jnp.where(rank < float(min(SLC_TOPK, nsel)), 0.0, NEG)
    bias_t = jnp.concatenate([jnp.zeros((HEAD_DIM, tq), F32), bias_t,
                              jnp.zeros((LANES - HEAD_DIM - nsel, tq), F32)], axis=0)
    bias = bias_t.T
    qs = (q128 + jnp.concatenate([bias] * B_GROUP, axis=0)).astype(BF16)

    def values_t(ref, first_blk, nblk):
        return jnp.concatenate([ref[first_blk + u] for u in range(nblk)], axis=1)

    def lead(nkeys, shift):
        return off_ref[0:nkeys, :] + shift

    not_after = off_ref[0:tq, :] >= 0

    def own_tile(k_own, qop, vt_own):
        s = jnp.where(not_after, _dot_nt(k_own, qop), NEG)
        m_ref = jnp.max(s, axis=0, keepdims=True)
        return m_ref, _dot(vt_own, jnp.exp2(s - m_ref).astype(BF16))

    def weighted_values(s, vt, m_ref, mask):
        if mask is not None:
            s = jnp.where(mask, s, NEG)
        return _dot(vt, jnp.exp2(s - m_ref).astype(BF16))

    blk_per_tile = tq // KEY_BLK
    m_ref_w, acc_w = own_tile(kv_ref[pl.ds(pl.multiple_of(t0, tq), tq), LANES:], qw,
                              values_t(vwt_ref, qi * blk_per_tile, blk_per_tile))
    for back in range(1, WIN // tq + 1):
        first = t0 - back * tq
        start = jnp.maximum(first, 0)
        shift = jnp.where(first >= 0, back * tq, 2 * WIN + tq)
        inside = lax.bitcast_convert_type(lead(tq, shift), jnp.uint32) < jnp.uint32(WIN)
        acc_w = acc_w + weighted_values(_dot_nt(kv_ref[pl.ds(pl.multiple_of(start, tq), tq), LANES:], qw),
                                        values_t(vwt_ref, start // KEY_BLK, blk_per_tile), m_ref_w, inside)

    def slc_tiles(own):
        def run():
            k0 = own * tq
            m_ref, acc = own_tile(kaug_ref[k0:k0 + tq, :], qs, values_t(vst_ref, own * blk_per_tile, blk_per_tile))
            for first, nkeys in [(j * NSA_TK, NSA_TK) for j in range(k0 // NSA_TK)] + (
                    [(k0 - tq, tq)] if k0 % NSA_TK else []):
                acc = acc + weighted_values(_dot_nt(kaug_ref[first:first + nkeys, :], qs),
                                            values_t(vst_ref, first // KEY_BLK, nkeys // KEY_BLK), m_ref, None)
            return acc
        return run

    acc_s = lax.switch(qi, [slc_tiles(n) for n in range(seq // tq)])

    gate_t = jax.nn.sigmoid(zg_ref[...]).T

    def grow(br):
        return jnp.concatenate([gate_t[3 * g + br:3 * g + br + 1, :] for g in range(B_GROUP)], axis=1)

    def gated_sum(a_s, a_w):
        return acc_c * grow(0) + a_s * (grow(1) / a_s[0:1, :]) + a_w * (grow(2) / a_w[0:1, :])

    def running_max_path():
        w0 = jnp.maximum(t0 - WIN, 0)
        wrows = pl.ds(pl.multiple_of(w0, tq), WIN + tq)
        in_window = lax.bitcast_convert_type(lead(WIN + tq, t0 - w0), jnp.uint32) < jnp.uint32(WIN)
        _, a_w = _softmax_step_t(_dot_nt(kv_ref[wrows, LANES:], qw),
                                 values_t(vwt_ref, w0 // KEY_BLK, (WIN + tq) // KEY_BLK), *init, mask=in_window)
        nblk = NSA_TK // KEY_BLK

        def body(j, carry):
            rows = pl.ds(pl.multiple_of(j * NSA_TK, NSA_TK), NSA_TK)
            return _softmax_step_t(_dot_nt(kaug_ref[rows, :], qs), values_t(vst_ref, j * nblk, nblk),
                                   *carry, mask=lead(NSA_TK, t0 - j * NSA_TK) >= 0)

        _, a_s = lax.fori_loop(0, t0 // NSA_TK + 1, body, init)
        return gated_sum(a_s, a_w)

    y_t = gated_sum(acc_s, acc_w)
    finite = jnp.logical_and(jnp.isfinite(y_t[HEAD_DIM:, :]),
                             jnp.isfinite(acc_s[0:1, :] + acc_w[0:1, :]))
    overflowed = jnp.max(jnp.where(finite, 0.0, 1.0)) > 0.0
    y_t = lax.cond(overflowed, running_max_path, lambda: y_t)
    for pair in range(2):
        even = y_t[:, (2 * pair) * tq:(2 * pair + 1) * tq].T
        odd = y_t[:, (2 * pair + 1) * tq:(2 * pair + 2) * tq].T
        o_ref[:, pair * LANES:(pair + 1) * LANES] = jnp.where(
            low, pltpu.roll(even, HEAD_DIM, 1), odd).astype(o_ref.dtype)


def _nsa(zq, zkv, kvc, zg, ovt):
    b, seq, _ = zq.shape
    ncmp = kvc.shape[2]
    nsel = seq // SLC_BLOCK
    return pl.pallas_call(
        _nsa_kernel,
        grid=(b, B_KV_HEADS, seq // NSA_TQ),
        in_specs=[pl.BlockSpec((None, NSA_TQ, MXU_DIM), lambda i, h, q: (i, q, h)),
                  pl.BlockSpec((None, seq, MXU_DIM), lambda i, h, q: (i, 0, h)),
                  pl.BlockSpec((None, None, ncmp, LANES), lambda i, h, q: (i, h, 0, 0)),
                  pl.BlockSpec((None, NSA_TQ, LANES), lambda i, h, q: (i, q, h)),
                  pl.BlockSpec((nsel, ncmp), lambda i, h, q: (0, 0))],
        out_specs=pl.BlockSpec((None, NSA_TQ, MXU_DIM), lambda i, h, q: (i, q, h)),
        out_shape=jax.ShapeDtypeStruct((b, seq, B_Q), BF16),
        scratch_shapes=[pltpu.VMEM((seq, LANES), BF16)]
        + [pltpu.VMEM((seq // KEY_BLK, LANES, KEY_BLK), BF16)] * 2
        + [pltpu.VMEM((WIN + NSA_TQ, B_GROUP * NSA_TQ), jnp.int32),
           pltpu.VMEM((ncmp, B_GROUP * NSA_TQ), jnp.int32)],
        compiler_params=_params("parallel", "parallel", "arbitrary"),
        name="nsa",
    )(zq, zkv, kvc, zg, ovt)


def _merge_kernel(h_ref, ya_ref, yb_ref, wg_ref, wba_ref, wbb_ref, wmix_ref, g_ref, b_ref, o_ref):
    h = h_ref[...]
    hb = h.astype(BF16)
    ya = ya_ref[...]
    yb = yb_ref[...]
    nchunk = D_MODEL // MXU_DIM

    def pre(n):
        cols = slice(n * MXU_DIM, (n + 1) * MXU_DIM)
        gcols = slice(D_MODEL + n * MXU_DIM, D_MODEL + (n + 1) * MXU_DIM)
        return (_dot(hb, wg_ref[:, cols]), _dot(ya, wba_ref[:, cols]),
                _dot(hb, wg_ref[:, gcols]), _dot(yb, wbb_ref[:, cols]))

    ahead = [pre(n) for n in range(2)]
    mix = jnp.zeros(h.shape, F32)
    for n in range(nchunk):
        za, pa, zb, pb = ahead.pop(0)
        if n + 2 < nchunk:
            ahead.append(pre(n + 2))
        merged = jax.nn.sigmoid(za) * pa + jax.nn.sigmoid(zb) * pb
        mix = mix + _dot(merged.astype(BF16), wmix_ref[n * MXU_DIM:(n + 1) * MXU_DIM, :])
    o_ref[...] = _layer_norm(ALPHA * h + mix, g_ref[...], b_ref[...])


def _merge(h2d, ya, yb, wg, wba, wbb, wmix, g, b, tm):
    t = h2d.shape[0]
    row = lambda i: (i, 0)
    const = lambda i: (0, 0)
    return pl.pallas_call(
        _merge_kernel,
        grid=(t // tm,),
        in_specs=[pl.BlockSpec((tm, D_MODEL), row), pl.BlockSpec((tm, A_OUT), row),
                  pl.BlockSpec((tm, B_Q), row), pl.BlockSpec((D_MODEL, 2 * D_MODEL), const),
                  pl.BlockSpec((A_OUT, D_MODEL), const), pl.BlockSpec((B_Q, D_MODEL), const),
                  pl.BlockSpec((D_MODEL, D_MODEL), const), pl.BlockSpec((1, D_MODEL), const),
                  pl.BlockSpec((1, D_MODEL), const)],
        out_specs=pl.BlockSpec((tm, D_MODEL), row),
        out_shape=jax.ShapeDtypeStruct((t, D_MODEL), F32),
        compiler_params=_params("parallel"),
        name="merge",
    )(h2d, ya, yb, wg, wba, wbb, wmix, g, b)


def _xkv_kernel(mem_ref, w_ref, o_ref):
    o_ref[...] = _dot(mem_ref[...].astype(BF16), w_ref[...]).astype(o_ref.dtype)


def _xkv(mem, w):
    b, mlen, _ = mem.shape
    return pl.pallas_call(
        _xkv_kernel,
        grid=(b,),
        in_specs=[pl.BlockSpec((None, mlen, D_MODEL), lambda i: (i, 0, 0)),
                  pl.BlockSpec((D_MODEL, 2 * X_DIM), lambda i: (0, 0))],
        out_specs=pl.BlockSpec((None, mlen, 2 * X_DIM), lambda i: (i, 0, 0)),
        out_shape=jax.ShapeDtypeStruct((b, mlen, 2 * X_DIM), BF16),
        compiler_params=_params("parallel"),
        name="xkv",
    )(mem, w)


X_ROW_PARTS = 2


def _xattn_kernel(h_ref, kv_ref, wq_ref, wo_ref, g_ref, b_ref, o_ref):
    tm = h_ref.shape[0]
    k = kv_ref[:, :X_DIM]
    v = kv_ref[:, X_DIM:]
    part = tm // X_ROW_PARTS
    lane = lax.broadcasted_iota(jnp.int32, (part, X_DIM), 1)
    for r in range(X_ROW_PARTS):
        rows = slice(r * part, (r + 1) * part)
        h = h_ref[rows, :]
        q = _dot(h.astype(BF16), wq_ref[...]) * (SCALE * LOG2E)
        q4 = jnp.concatenate([jnp.where((lane // HEAD_DIM) == hd, q, 0.0) for hd in range(X_HEADS)],
                             axis=0).astype(BF16)
        s = _dot_nt(q4, k)
        e = jnp.exp2(s - jnp.max(s, axis=-1, keepdims=True))
        pv = _dot(e.astype(BF16), v) / jnp.sum(e, axis=-1, keepdims=True)
        o = pv[:part]
        for hd in range(1, X_HEADS):
            o = jnp.where((lane // HEAD_DIM) == hd, pv[hd * part:(hd + 1) * part], o)
        att = _dot(o.astype(BF16), wo_ref[...])
        o_ref[rows, :] = _layer_norm(ALPHA * h + att, g_ref[...], b_ref[...])


def _xattn(h2d, kv, wq, wo, g, b, seq, tm):
    t = h2d.shape[0]
    mlen = kv.shape[1]
    nseq = seq // tm
    row = lambda i: (i, 0)
    const = lambda i: (0, 0)
    return pl.pallas_call(
        _xattn_kernel,
        grid=(t // tm,),
        in_specs=[pl.BlockSpec((tm, D_MODEL), row),
                  pl.BlockSpec((None, mlen, 2 * X_DIM), lambda i: (i // nseq, 0, 0)),
                  pl.BlockSpec((D_MODEL, X_DIM), const), pl.BlockSpec((X_DIM, D_MODEL), const),
                  pl.BlockSpec((1, D_MODEL), const), pl.BlockSpec((1, D_MODEL), const)],
        out_specs=pl.BlockSpec((tm, D_MODEL), row),
        out_shape=jax.ShapeDtypeStruct((t, D_MODEL), F32),
        compiler_params=_params("parallel"),
        name="xattn",
    )(h2d, kv, wq, wo, g, b)


FFN_CHUNK = 256
FFN_AHEAD = 2
FFN_DOWN_GROUP = 6
FFN_ROW_PARTS = 1


def _ffn_kernel(h_ref, wup_ref, cw_ref, cb_ref, wdn_ref, g_ref, b_ref, o_ref, tail_sc, *, tiles_per_seq):
    tm = h_ref.shape[0]
    i = pl.program_id(0)

    @pl.when(i % tiles_per_seq == 0)
    def _():
        tail_sc[...] = jnp.zeros(tail_sc.shape, F32)

    nchunk = D_FF // FFN_CHUNK
    part = tm // FFN_ROW_PARTS
    top = lax.broadcasted_iota(jnp.int32, (SUBLANES, FFN_CHUNK), 0)

    def delayed(u, tail, d):
        ur = pltpu.roll(u, d, 0)
        first = jnp.where(top < d, pltpu.roll(tail, d, 0), ur[:SUBLANES])
        return jnp.concatenate([first, ur[SUBLANES:]], axis=0)

    for r in range(FFN_ROW_PARTS):
        rows = slice(r * part, (r + 1) * part)
        h = h_ref[rows, :]
        hb = h.astype(BF16)
        acc = jnp.zeros((part, D_MODEL), F32)

        def up(c, hb=hb):
            return [_dot(hb, wup_ref[:, base:base + FFN_CHUNK]) for base in (c * FFN_CHUNK, D_FF + c * FFN_CHUNK)]

        ahead = [up(c) for c in range(FFN_AHEAD)]
        acts = []
        for c in range(nchunk):
            u_cur = ahead.pop(0)
            if c + FFN_AHEAD < nchunk:
                ahead.append(up(c + FFN_AHEAD))
            halves = []
            for u, base in zip(u_cur, (c * FFN_CHUNK, D_FF + c * FFN_CHUNK)):
                cols = slice(base, base + FFN_CHUNK)
                tail = tail_sc[:, cols]
                tail_sc[:, cols] = u[part - SUBLANES:]
                halves.append(cw_ref[2:3, cols] * u + cw_ref[1:2, cols] * delayed(u, tail, 1)
                              + cw_ref[0:1, cols] * delayed(u, tail, 2) + cb_ref[:, cols])
            acts.append((_gelu_tanh(halves[0]) * halves[1]).astype(BF16))
            if len(acts) == FFN_DOWN_GROUP or c == nchunk - 1:
                k0 = (c + 1 - len(acts)) * FFN_CHUNK
                acc = acc + _dot(jnp.concatenate(acts, axis=1), wdn_ref[k0:(c + 1) * FFN_CHUNK, :])
                acts = []
        o_ref[rows, :] = _layer_norm(ALPHA * h + acc, g_ref[...], b_ref[...])


def _ffn(h2d, wup, cw, cb, wdn, g, b, seq, tm):
    t = h2d.shape[0]
    row = lambda i: (i, 0)
    const = lambda i: (0, 0)
    return pl.pallas_call(
        functools.partial(_ffn_kernel, tiles_per_seq=seq // tm),
        grid=(t // tm,),
        in_specs=[pl.BlockSpec((tm, D_MODEL), row),
                  pl.BlockSpec((D_MODEL, 2 * D_FF), const, pipeline_mode=pl.Buffered(1)),
                  pl.BlockSpec((CONV_W, 2 * D_FF), const), pl.BlockSpec((1, 2 * D_FF), const),
                  pl.BlockSpec((D_FF, D_MODEL), const, pipeline_mode=pl.Buffered(1)),
                  pl.BlockSpec((1, D_MODEL), const), pl.BlockSpec((1, D_MODEL), const)],
        out_specs=pl.BlockSpec((tm, D_MODEL), row),
        out_shape=jax.ShapeDtypeStruct((t, D_MODEL), F32),
        scratch_shapes=[pltpu.VMEM((SUBLANES, 2 * D_FF), F32)],
        compiler_params=_params("arbitrary"),
        name="ffn",
    )(h2d, wup, cw, cb, wdn, g, b)


def _rope_tables(seq):
    inv = ROPE_THETA ** (-jnp.arange(HALF, dtype=F32) / HALF)
    ang = jnp.arange(seq, dtype=F32)[:, None] * inv[None, :]
    cos, sin = jnp.cos(ang), jnp.sin(ang)
    zero, one = jnp.zeros_like(cos), jnp.ones_like(cos)
    cq = jnp.concatenate([cos, cos, cos, cos], axis=1)
    s1q = jnp.concatenate([-sin, zero, -sin, zero], axis=1)
    s2q = jnp.concatenate([zero, sin, zero, sin], axis=1)
    ck = jnp.concatenate([cos, cos, one, one], axis=1)
    s1k = jnp.concatenate([-sin, zero, zero, zero], axis=1)
    s2k = jnp.concatenate([zero, sin, zero, zero], axis=1)
    pos_c = (CMP_STRIDE * jnp.arange(seq // CMP_STRIDE) + CMP_LEN - 1).astype(F32)
    ang_c = pos_c[:, None] * inv[None, :]
    return (cq, s1q, s2q, ck, s1k, s2k), jnp.cos(ang_c), jnp.sin(ang_c)


def _overlap_t(seq):
    ncmp = seq // CMP_STRIDE
    nsel = seq // SLC_BLOCK
    c_start = CMP_STRIDE * jnp.arange(ncmp)
    s_start = SLC_BLOCK * jnp.arange(nsel)
    ov = jnp.clip(jnp.minimum(c_start[None, :] + CMP_LEN, s_start[:, None] + SLC_BLOCK)
                  - jnp.maximum(c_start[None, :], s_start[:, None]), 0).astype(F32) / CMP_LEN
    return ov.astype(BF16)


def _pack_w_in(w):
    o1 = A_QKV
    o2 = o1 + B_Q
    o3 = o2 + B_KV
    o4 = o3 + B_GATE
    wa, wq, wkv, wbg, wgate = w[:, :o1], w[:, o1:o2], w[:, o2:o3], w[:, o3:o4], w[:, o4:]
    kv = wkv.reshape(D_MODEL, 3, 2, B_KV_HEADS, HEAD_DIM)
    per_head = [jnp.concatenate([kv[:, 1, 0, hh], kv[:, 1, 1, hh], kv[:, 2, 0, hh], kv[:, 2, 1, hh]], axis=1)
                for hh in range(B_KV_HEADS)]
    cmp_blk = wkv[:, :2 * B_KV_HEADS * HEAD_DIM]
    per_gate = B_GROUP * 3
    gates = [jnp.pad(wbg[:, hh * per_gate:(hh + 1) * per_gate], ((0, 0), (0, LANES - per_gate)))
             for hh in range(B_KV_HEADS)]
    wp = jnp.concatenate([wa, wq] + per_head + [cmp_blk] + gates, axis=1)
    return wp.astype(BF16), wgate.astype(BF16)


def kernel(x, mem, w_in, cmp_pos, cmp_w1, cmp_b1, cmp_w2, w_branch_a, w_branch_b, w_mix_out, ln1_g, ln1_b,
           w_xq, w_xkv, w_xo, ln2_g, ln2_b, w_up, conv_w, conv_b, w_down, ln3_g, ln3_b):
    b, seq, _ = x.shape
    t = b * seq
    tabs, cos_c, sin_c = _rope_tables(seq)
    ovt = _overlap_t(seq)
    h = x.reshape(t, D_MODEL)
    for l in range(DEPTH):
        wp, wgate = _pack_w_in(w_in[l])
        za, zq, zkv, zc, zg = _project(h, wp, tabs, seq, tm=512)
        kvc = _compress(zc.reshape(b, seq, MXU_DIM),
                        cmp_w1[l].reshape(2, CMP_LEN, HEAD_DIM, CMP_HIDDEN).astype(BF16),
                        cmp_pos[l], cmp_b1[l], cmp_w2[l].astype(BF16), cos_c, sin_c)
        ya = _attn_a(za.reshape(b, seq, A_QKV))
        yb = _nsa(zq.reshape(b, seq, B_Q), zkv.reshape(b, seq, 2 * MXU_DIM), kvc,
                  zg.reshape(b, seq, MXU_DIM), ovt)
        h = _merge(h, ya.reshape(t, A_OUT), yb.reshape(t, B_Q), wgate, w_branch_a[l].astype(BF16),
                   w_branch_b[l].astype(BF16), w_mix_out[l].astype(BF16),
                   ln1_g[l][None, :], ln1_b[l][None, :], tm=512)
        xkv = _xkv(mem, w_xkv[l].astype(BF16))
        h = _xattn(h, xkv, w_xq[l].astype(BF16), w_xo[l].astype(BF16),
                   ln2_g[l][None, :], ln2_b[l][None, :], seq, tm=512)
        h = _ffn(h, w_up[l].astype(BF16), conv_w[l], conv_b[l][None, :], w_down[l].astype(BF16),
                 ln3_g[l][None, :], ln3_b[l][None, :], seq, tm=512)
    return h.reshape(b, seq, D_MODEL)
```

```python
import functools

import jax
import jax.numpy as jnp
from jax import lax
from jax.experimental import pallas as pl
from jax.experimental.pallas import tpu as pltpu

F32 = jnp.float32
BF16 = jnp.bfloat16

D_MODEL = 1024
DEPTH = 2
HEAD_DIM = 64
HALF = HEAD_DIM // 2
ROPE_THETA = 10000.0
LN_EPS = 1e-5
ALPHA = (2 * DEPTH) ** 0.25
SCALE = HEAD_DIM ** -0.5

DIL_PATTERNS = ((128, 1), (512, 4), (2048, 16))
A_GROUPS = 3
A_HEADS_PER_GROUP = 4
A_QKV = 3 * A_GROUPS * A_HEADS_PER_GROUP * HEAD_DIM
A_OUT = A_HEADS_PER_GROUP * HEAD_DIM

B_Q_HEADS = 8
B_KV_HEADS = 2
B_GROUP = 4
B_Q = B_Q_HEADS * HEAD_DIM
B_KV = 3 * 2 * B_KV_HEADS * HEAD_DIM
B_GATE = 3 * B_Q_HEADS
CMP_STRIDE = 16
CMP_LEN = 32
CMP_HIDDEN = 256
SLC_BLOCK = 64
SLC_TOPK = 8
WIN = 512

X_HEADS = 4
X_DIM = X_HEADS * HEAD_DIM
D_FF = 2816
CONV_W = 3

LANES = 128
SUBLANES = 8
MXU_DIM = 256
VMEM_LIMIT_BYTES = 56 * 1024 * 1024

NEG = -1e30
LOG2E = 1.4426950408889634

N_PROJ_BLOCKS = 15
PROJ_COLS = N_PROJ_BLOCKS * MXU_DIM
PROJ_AHEAD = 2

_NT = (((1,), (1,)), ((), ()))


def _dot(a, b):
    return jnp.dot(a, b, preferred_element_type=F32)


def _dot_nt(a, b):
    return lax.dot_general(a, b, _NT, preferred_element_type=F32)


def _params(*sem):
    return pltpu.CompilerParams(dimension_semantics=sem, vmem_limit_bytes=VMEM_LIMIT_BYTES)


def _layer_norm(x, g, b):
    mu = jnp.mean(x, axis=-1, keepdims=True)
    xc = x - mu
    var = jnp.mean(xc * xc, axis=-1, keepdims=True)
    return xc * lax.rsqrt(var + LN_EPS) * g + b


def _gelu_tanh(x):
    c = 0.7978845608028654
    half = 0.5 * x
    return half + half * jnp.tanh(x * (c + (c * 0.044715) * (x * x)))


def _rope128(z, c, s1, s2):
    return z * c + pltpu.roll(z, LANES - HALF, 1) * s1 + pltpu.roll(z, HALF, 1) * s2


def _proj_kernel(x_ref, w_ref, cq_ref, s1q_ref, s2q_ref, ck_ref, s1k_ref, s2k_ref,
                 za_ref, zq_ref, zkv_ref, zc_ref, zg_ref):
    xb = x_ref[...].astype(BF16)

    def block(j):
        return _dot(xb, w_ref[:, j * MXU_DIM:(j + 1) * MXU_DIM])

    ahead = [block(j) for j in range(PROJ_AHEAD)]
    for j in range(N_PROJ_BLOCKS):
        z = ahead.pop(0)
        if j + PROJ_AHEAD < N_PROJ_BLOCKS:
            ahead.append(block(j + PROJ_AHEAD))
        if j < 6 or 9 <= j <= 10:
            tabs = (cq_ref[...], s1q_ref[...], s2q_ref[...])
        elif 11 <= j <= 12:
            tabs = (ck_ref[...], s1k_ref[...], s2k_ref[...])
        else:
            tabs = None
        if tabs is not None:
            z = jnp.concatenate([_rope128(z[:, :LANES], *tabs), _rope128(z[:, LANES:], *tabs)], axis=1)
        if j < 9:
            za_ref[:, j * MXU_DIM:(j + 1) * MXU_DIM] = z
        elif j < 11:
            zq_ref[:, (j - 9) * MXU_DIM:(j - 8) * MXU_DIM] = z.astype(BF16)
        elif j < 13:
            zkv_ref[:, (j - 11) * MXU_DIM:(j - 10) * MXU_DIM] = z.astype(BF16)
        elif j == 13:
            zc_ref[...] = z
        else:
            zg_ref[...] = z


def _project(h2d, w, tabs, seq, tm):
    t = h2d.shape[0]
    nseq = seq // tm
    row = lambda i: (i, 0)
    tab_spec = pl.BlockSpec((tm, LANES), lambda i: (i % nseq, 0))
    return pl.pallas_call(
        _proj_kernel,
        grid=(t // tm,),
        in_specs=[pl.BlockSpec((tm, D_MODEL), row),
                  pl.BlockSpec((D_MODEL, PROJ_COLS), lambda i: (0, 0), pipeline_mode=pl.Buffered(1))]
        + [tab_spec] * 6,
        out_specs=[pl.BlockSpec((tm, A_QKV), row), pl.BlockSpec((tm, B_Q), row),
                   pl.BlockSpec((tm, 2 * MXU_DIM), row), pl.BlockSpec((tm, MXU_DIM), row),
                   pl.BlockSpec((tm, MXU_DIM), row)],
        out_shape=[jax.ShapeDtypeStruct((t, A_QKV), F32), jax.ShapeDtypeStruct((t, B_Q), BF16),
                   jax.ShapeDtypeStruct((t, 2 * MXU_DIM), BF16), jax.ShapeDtypeStruct((t, MXU_DIM), F32),
                   jax.ShapeDtypeStruct((t, MXU_DIM), F32)],
        compiler_params=_params("parallel"),
        name="proj",
    )(h2d, w, *tabs)


def _cmp_kernel(zc_ref, w1_ref, pe_ref, b1_ref, w2_ref, cos_ref, sin_ref, o_ref, xk_ref, xv_ref):
    seq = zc_ref.shape[0]
    nblk = seq // CMP_STRIDE
    xs_refs = (xk_ref, xv_ref)
    for kv in range(2):
        xs_refs[kv][0:seq, :] = zc_ref[:, kv * LANES:(kv + 1) * LANES]
        xs_refs[kv][seq:seq + CMP_LEN, :] = jnp.zeros((CMP_LEN, LANES), F32)
    acc = [jnp.zeros((nblk, CMP_HIDDEN), F32) for _ in range(4)]
    for j in range(CMP_LEN):
        xj = [r[pl.ds(j, nblk, stride=CMP_STRIDE), :] for r in xs_refs]
        for c in range(4):
            kv, hh = c // 2, c % 2
            piece = xj[kv][:, hh * HEAD_DIM:(hh + 1) * HEAD_DIM] + pe_ref[kv, j:j + 1, :]
            acc[c] = acc[c] + _dot(piece.astype(BF16), w1_ref[kv, j])
    outs = []
    for c in range(4):
        kv = c // 2
        hid = jax.nn.gelu(acc[c] + b1_ref[kv:kv + 1, :])
        y = _dot(hid.astype(BF16), w2_ref[kv])
        if kv == 0:
            y1, y2 = y[:, :HALF], y[:, HALF:]
            cs, sn = cos_ref[...], sin_ref[...]
            y = jnp.concatenate([y1 * cs - y2 * sn, y1 * sn + y2 * cs], axis=1)
        outs.append(y)
    o_ref[0] = jnp.concatenate([outs[0], outs[2]], axis=1)
    o_ref[1] = jnp.concatenate([outs[1], outs[3]], axis=1)


def _compress(zc, w1, pe, b1, w2, cos_c, sin_c):
    b, seq, _ = zc.shape
    nblk = seq // CMP_STRIDE
    full = lambda *shape: pl.BlockSpec(shape, lambda i: (0,) * len(shape))
    return pl.pallas_call(
        _cmp_kernel,
        grid=(b,),
        in_specs=[pl.BlockSpec((None, seq, MXU_DIM), lambda i: (i, 0, 0)),
                  full(2, CMP_LEN, HEAD_DIM, CMP_HIDDEN), full(2, CMP_LEN, HEAD_DIM),
                  full(2, CMP_HIDDEN), full(2, CMP_HIDDEN, HEAD_DIM),
                  full(nblk, HALF), full(nblk, HALF)],
        out_specs=pl.BlockSpec((None, 2, nblk, LANES), lambda i: (i, 0, 0, 0)),
        out_shape=jax.ShapeDtypeStruct((b, 2, nblk, LANES), F32),
        scratch_shapes=[pltpu.VMEM((seq + CMP_LEN, LANES), F32), pltpu.VMEM((seq + CMP_LEN, LANES), F32)],
        compiler_params=_params("parallel"),
        name="compress",
    )(zc, w1, pe, b1, w2, cos_c, sin_c)


A_TILE = 128
A_UNROLL = 8
FREE_STRIDE = 4


def _attn_a_kernel(q0, k0, v0, q1, k1, v1, q2, k2, v2, y_ref, o_sc, l_sc, tq_sc, tk_sc, tv_sc, to_sc, tl_sc):
    seq = y_ref.shape[0]
    qkv = ((q0, k0, v0), (q1, k1, v1), (q2, k2, v2))
    first_head = lax.broadcasted_iota(jnp.int32, (A_TILE, LANES), 1) < HEAD_DIM

    for g, (_, dil) in enumerate(DIL_PATTERNS):
        q_ref, k_ref, v_ref = qkv[g]
        o_dst, l_dst = o_sc.at[g], l_sc.at[g]
        ntile = seq // dil // A_TILE
        span = A_TILE * dil
        nkeys = 2 * A_TILE if ntile > 1 else A_TILE
        row = lax.broadcasted_iota(jnp.int32, (A_TILE, nkeys), 0)
        col = lax.broadcasted_iota(jnp.int32, (A_TILE, nkeys), 1)
        band = jnp.logical_and(col >= row, col <= row + A_TILE) if ntile > 1 else col <= row
        own = col >= A_TILE
        two_stage = dil > FREE_STRIDE
        if two_stage:
            assert ntile == 1 and dil % FREE_STRIDE == 0
            part = seq // FREE_STRIDE
            srcs = (q_ref, k_ref, v_ref)
            q_ref, k_ref, v_ref, o_dst, l_dst = tq_sc, tk_sc, tv_sc, to_sc, tl_sc
            for src, dst in zip(srcs, (q_ref, k_ref, v_ref)):
                for r4 in range(FREE_STRIDE):
                    dst[r4 * part:(r4 + 1) * part, :] = src[pl.ds(r4, part, stride=FREE_STRIDE), :]

        def rows_at(start, dil=dil, two_stage=two_stage):
            if dil == 1:
                return pl.ds(pl.multiple_of(start, A_TILE), A_TILE)
            if two_stage:
                return pl.ds((start % FREE_STRIDE) * (seq // FREE_STRIDE) + start // FREE_STRIDE, A_TILE,
                             stride=dil // FREE_STRIDE)
            return pl.ds(start, A_TILE, stride=dil)

        def load(n, q_ref=q_ref, k_ref=k_ref, v_ref=v_ref, ntile=ntile, span=span, rows_at=rows_at,
                 band=band, own=own):
            r = n // ntile
            i = n % ntile
            start = r + i * span
            rows = rows_at(start)
            q = q_ref[rows, :] * (SCALE * LOG2E)
            q2 = jnp.concatenate([jnp.where(first_head, q, 0.0), jnp.where(first_head, 0.0, q)], axis=0).astype(BF16)
            k = k_ref[rows, :].astype(BF16)
            v = v_ref[rows, :].astype(BF16)
            valid = band
            if ntile > 1:
                prows = rows_at(jnp.maximum(start - span, r))
                k = jnp.concatenate([k_ref[prows, :].astype(BF16), k], axis=0)
                v = jnp.concatenate([v_ref[prows, :].astype(BF16), v], axis=0)
                valid = jnp.logical_and(band, jnp.logical_or(own, i > 0))
            return rows, q2, k, v, jnp.concatenate([valid, valid], axis=0)

        def body(it, carry, load=load, o_dst=o_dst, l_dst=l_dst):
            tiles = [load(it * A_UNROLL + u) for u in range(A_UNROLL)]
            scores = [jnp.where(valid, _dot_nt(q2, k), NEG) for _, q2, k, _, valid in tiles]
            for (rows, _, _, v, _), s in zip(tiles, scores):
                m = jnp.max(s, axis=-1, keepdims=True)
                e = jnp.exp2(s - m)
                l = jnp.sum(e, axis=-1, keepdims=True)
                o = _dot(e.astype(BF16), v) / l
                lse = jnp.broadcast_to(jnp.log2(l) + m, o.shape)
                o_dst[rows, :] = jnp.where(first_head, o[:A_TILE], o[A_TILE:])
                l_dst[rows, :] = jnp.where(first_head, lse[:A_TILE], lse[A_TILE:])
            return carry

        lax.fori_loop(0, seq // A_TILE // A_UNROLL, body, 0)
        if two_stage:
            for src, dst in ((o_dst, o_sc.at[g]), (l_dst, l_sc.at[g])):
                for r4 in range(FREE_STRIDE):
                    dst[pl.ds(r4, part, stride=FREE_STRIDE), :] = src[r4 * part:(r4 + 1) * part, :]

    chunk = 256
    for c in range(seq // chunk):
        rows = pl.ds(c * chunk, chunk)
        ls = [l_sc[g, rows, :] for g in range(A_GROUPS)]
        mx = jnp.maximum(jnp.maximum(ls[0], ls[1]), ls[2])
        ws = [jnp.exp2(l - mx) for l in ls]
        num = ws[0] * o_sc[0, rows, :] + ws[1] * o_sc[1, rows, :] + ws[2] * o_sc[2, rows, :]
        y_ref[rows, :] = (num / (ws[0] + ws[1] + ws[2])).astype(y_ref.dtype)


def _attn_a(za):
    b, seq, _ = za.shape
    in_specs = []
    for g in range(A_GROUPS):
        for part in range(3):
            blk = part * 2 * A_GROUPS + 2 * g
            in_specs.append(pl.BlockSpec((None, seq, LANES), lambda i, hp, blk=blk: (i, 0, blk + hp)))
    return pl.pallas_call(
        _attn_a_kernel,
        grid=(b, 2),
        in_specs=in_specs,
        out_specs=pl.BlockSpec((None, seq, LANES), lambda i, hp: (i, 0, hp)),
        out_shape=jax.ShapeDtypeStruct((b, seq, A_OUT), BF16),
        scratch_shapes=[pltpu.VMEM((A_GROUPS, seq, LANES), F32), pltpu.VMEM((A_GROUPS, seq, LANES), F32),
                        ] + [pltpu.VMEM((seq, LANES), F32)] * 5,
        compiler_params=_params("parallel", "parallel"),
        name="attn_a",
    )(*([za] * 9))


NSA_TQ = 256
NSA_TK = 512
KEY_BLK = 128


def _softmax_step_t(s, vt, m, acc, mask=None):
    if mask is not None:
        s = jnp.where(mask, s, NEG)
    m_new = jnp.maximum(m, jnp.max(s, axis=0, keepdims=True))
    acc = jnp.exp2(m - m_new) * acc + _dot(vt, jnp.exp2(s - m_new).astype(BF16))
    return m_new, acc


def _nsa_kernel(q_ref, kv_ref, kvc_ref, zg_ref, ovt_ref, o_ref, kaug_ref, vst_ref, vwt_ref, off_ref, coff_ref):
    seq = kv_ref.shape[0]
    nsel = seq // SLC_BLOCK
    ncmp = kvc_ref.shape[0]
    tq = NSA_TQ
    mcols = B_GROUP * tq
    qi = pl.program_id(2)
    t0 = qi * tq
    upper = lax.broadcasted_iota(jnp.int32, (LANES, KEY_BLK), 0) < HEAD_DIM

    @pl.when(qi == 0)
    def _():
        key = lax.broadcasted_iota(jnp.int32, (seq, LANES), 0)
        lane = lax.broadcasted_iota(jnp.int32, (seq, LANES), 1)
        onehot = jnp.logical_and(lane >= HEAD_DIM, (key // SLC_BLOCK) == lane - HEAD_DIM)
        kaug_ref[...] = jnp.where(lane < HEAD_DIM, kv_ref[:, :LANES],
                                  jnp.where(onehot, 1.0, 0.0).astype(BF16))
        off_ref[...] = (lax.broadcasted_iota(jnp.int32, off_ref.shape, 1) % tq
                        - lax.broadcasted_iota(jnp.int32, off_ref.shape, 0))
        coff_ref[...] = (lax.broadcasted_iota(jnp.int32, coff_ref.shape, 1) % tq
                         - CMP_STRIDE * lax.broadcasted_iota(jnp.int32, coff_ref.shape, 0))
        for kb in range(seq // KEY_BLK):
            rows = slice(kb * KEY_BLK, (kb + 1) * KEY_BLK)
            vst_ref[kb] = jnp.where(upper, 1.0, kv_ref[rows, :LANES].astype(F32).T).astype(BF16)
            vwt_ref[kb] = jnp.where(upper, 1.0, kv_ref[rows, LANES:].astype(F32).T).astype(BF16)

    lane_q = lax.broadcasted_iota(jnp.int32, (tq, LANES), 1)
    low = lane_q < HEAD_DIM
    qf = q_ref[...].astype(F32) * (SCALE * LOG2E)
    parts = []
    for pair in range(2):
        x = qf[:, pair * LANES:(pair + 1) * LANES]
        parts.append(jnp.where(low, x, 0.0))
        parts.append(jnp.where(low, pltpu.roll(x, HEAD_DIM, 1), 0.0))
    q128 = jnp.concatenate(parts, axis=0)
    qw = q128.astype(BF16)

    init = (jnp.full((1, mcols), NEG, F32), jnp.zeros((LANES, mcols), F32))

    kvc = kvc_ref[...]
    s = _dot_nt(kvc.astype(BF16), qw)
    s = jnp.where(coff_ref[...] >= (CMP_LEN - 1) - t0, s, -jnp.inf)
    m = jnp.max(s, axis=0, keepdims=True)
    m = jnp.where(m == -jnp.inf, 0.0, m)
    e = jnp.exp2(s - m)
    l_c = jnp.maximum(jnp.sum(e, axis=0, keepdims=True), 1e-30)
    p = e / l_c
    acc_c = _dot(kvc.T.astype(BF16), p.astype(BF16))

    psum = p[:, 0:tq] + p[:, tq:2 * tq] + p[:, 2 * tq:3 * tq] + p[:, 3 * tq:4 * tq]
    p_hi = psum.astype(BF16)
    r1 = psum - p_hi.astype(F32)
    p_mid = r1.astype(BF16)
    p_lo = (r1 - p_mid.astype(F32)).astype(BF16)
    ovt = ovt_ref[...]
    imp = _dot(ovt, p_hi) + _dot(ovt, p_mid) + _dot(ovt, p_lo)
    blk = lax.broadcasted_iota(jnp.int32, (nsel, tq), 0)
    tcol = t0 + lax.broadcasted_iota(jnp.int32, (nsel, tq), 1)
    cur = tcol // SLC_BLOCK
    forced = jnp.logical_or(blk == 0, jnp.logical_or(blk == cur, blk == cur - 1))
    future = blk * SLC_BLOCK > tcol
    imp = jnp.where(future, -jnp.inf, jnp.where(forced, jnp.inf, imp))
    rank = jnp.zeros((nsel, tq), F32)
    for mp in range(nsel):
        other = imp[mp:mp + 1, :]
        before = jnp.logical_or(other > imp, jnp.logical_and(other == imp, blk > mp))
        rank = rank + jnp.where(before, 1.0, 0.0)
    bias_t = jnp.where(rank < float(min(SLC_TOPK, nsel)), 0.0, NEG)
    bias_t = jnp.concatenate([jnp.zeros((HEAD_DIM, tq), F32), bias_t,
                              jnp.zeros((LANES - HEAD_DIM - nsel, tq), F32)], axis=0)
    bias = bias_t.T
    qs = (q128 + jnp.concatenate([bias] * B_GROUP, axis=0)).astype(BF16)

    def values_t(ref, first_blk, nblk):
        return jnp.concatenate([ref[first_blk + u] for u in range(nblk)], axis=1)

    def lead(nkeys, shift):
        return off_ref[0:nkeys, :] + shift

    not_after = off_ref[0:tq, :] >= 0

    def own_tile(k_own, qop, vt_own):
        s = jnp.where(not_after, _dot_nt(k_own, qop), NEG)
        m_ref = jnp.max(s, axis=0, keepdims=True)
        return m_ref, _dot(vt_own, jnp.exp2(s - m_ref).astype(BF16))

    def weighted_values(s, vt, m_ref, mask):
        if mask is not None:
            s = jnp.where(mask, s, NEG)
        return _dot(vt, jnp.exp2(s - m_ref).astype(BF16))

    blk_per_tile = tq // KEY_BLK
    m_ref_w, acc_w = own_tile(kv_ref[pl.ds(pl.multiple_of(t0, tq), tq), LANES:], qw,
                              values_t(vwt_ref, qi * blk_per_tile, blk_per_tile))
    for back in range(1, WIN // tq + 1):
        first = t0 - back * tq
        start = jnp.maximum(first, 0)
        shift = jnp.where(first >= 0, back * tq, 2 * WIN + tq)
        inside = lax.bitcast_convert_type(lead(tq, shift), jnp.uint32) < jnp.uint32(WIN)
        acc_w = acc_w + weighted_values(_dot_nt(kv_ref[pl.ds(pl.multiple_of(start, tq), tq), LANES:], qw),
                                        values_t(vwt_ref, start // KEY_BLK, blk_per_tile), m_ref_w, inside)

    def slc_tiles(own):
        def run():
            k0 = own * tq
            m_ref, acc = own_tile(kaug_ref[k0:k0 + tq, :], qs, values_t(vst_ref, own * blk_per_tile, blk_per_tile))
            for first, nkeys in [(j * NSA_TK, NSA_TK) for j in range(k0 // NSA_TK)] + (
                    [(k0 - tq, tq)] if k0 % NSA_TK else []):
                acc = acc + weighted_values(_dot_nt(kaug_ref[first:first + nkeys, :], qs),
                                            values_t(vst_ref, first // KEY_BLK, nkeys // KEY_BLK), m_ref, None)
            return acc
        return run

    acc_s = lax.switch(qi, [slc_tiles(n) for n in range(seq // tq)])

    gate_t = jax.nn.sigmoid(zg_ref[...]).T

    def grow(br):
        return jnp.concatenate([gate_t[3 * g + br:3 * g + br + 1, :] for g in range(B_GROUP)], axis=1)

    def gated_sum(a_s, a_w):
        return acc_c * grow(0) + a_s * (grow(1) / a_s[0:1, :]) + a_w * (grow(2) / a_w[0:1, :])

    def running_max_path():
        w0 = jnp.maximum(t0 - WIN, 0)
        wrows = pl.ds(pl.multiple_of(w0, tq), WIN + tq)
        in_window = lax.bitcast_convert_type(lead(WIN + tq, t0 - w0), jnp.uint32) < jnp.uint32(WIN)
        _, a_w = _softmax_step_t(_dot_nt(kv_ref[wrows, LANES:], qw),
                                 values_t(vwt_ref, w0 // KEY_BLK, (WIN + tq) // KEY_BLK), *init, mask=in_window)
        nblk = NSA_TK // KEY_BLK

        def body(j, carry):
            rows = pl.ds(pl.multiple_of(j * NSA_TK, NSA_TK), NSA_TK)
            return _softmax_step_t(_dot_nt(kaug_ref[rows, :], qs), values_t(vst_ref, j * nblk, nblk),
                                   *carry, mask=lead(NSA_TK, t0 - j * NSA_TK) >= 0)

        _, a_s = lax.fori_loop(0, t0 // NSA_TK + 1, body, init)
        return gated_sum(a_s, a_w)

    y_t = gated_sum(acc_s, acc_w)
    finite = jnp.logical_and(jnp.isfinite(y_t[HEAD_DIM:, :]),
                             jnp.isfinite(acc_s[0:1, :] + acc_w[0:1, :]))
    overflowed = jnp.max(jnp.where(finite, 0.0, 1.0)) > 0.0
    y_t = lax.cond(overflowed, running_max_path, lambda: y_t)
    for pair in range(2):
        even = y_t[:, (2 * pair) * tq:(2 * pair + 1) * tq].T
        odd = y_t[:, (2 * pair + 1) * tq:(2 * pair + 2) * tq].T
        o_ref[:, pair * LANES:(pair + 1) * LANES] = jnp.where(
            low, pltpu.roll(even, HEAD_DIM, 1), odd).astype(o_ref.dtype)


def _nsa(zq, zkv, kvc, zg, ovt):
    b, seq, _ = zq.shape
    ncmp = kvc.shape[2]
    nsel = seq // SLC_BLOCK
    return pl.pallas_call(
        _nsa_kernel,
        grid=(b, B_KV_HEADS, seq // NSA_TQ),
        in_specs=[pl.BlockSpec((None, NSA_TQ, MXU_DIM), lambda i, h, q: (i, q, h)),
                  pl.BlockSpec((None, seq, MXU_DIM), lambda i, h, q: (i, 0, h)),
                  pl.BlockSpec((None, None, ncmp, LANES), lambda i, h, q: (i, h, 0, 0)),
                  pl.BlockSpec((None, NSA_TQ, LANES), lambda i, h, q: (i, q, h)),
                  pl.BlockSpec((nsel, ncmp), lambda i, h, q: (0, 0))],
        out_specs=pl.BlockSpec((None, NSA_TQ, MXU_DIM), lambda i, h, q: (i, q, h)),
        out_shape=jax.ShapeDtypeStruct((b, seq, B_Q), BF16),
        scratch_shapes=[pltpu.VMEM((seq, LANES), BF16)]
        + [pltpu.VMEM((seq // KEY_BLK, LANES, KEY_BLK), BF16)] * 2
        + [pltpu.VMEM((WIN + NSA_TQ, B_GROUP * NSA_TQ), jnp.int32),
           pltpu.VMEM((ncmp, B_GROUP * NSA_TQ), jnp.int32)],
        compiler_params=_params("parallel", "parallel", "arbitrary"),
        name="nsa",
    )(zq, zkv, kvc, zg, ovt)


def _merge_kernel(h_ref, ya_ref, yb_ref, wg_ref, wba_ref, wbb_ref, wmix_ref, g_ref, b_ref, o_ref):
    h = h_ref[...]
    hb = h.astype(BF16)
    ya = ya_ref[...]
    yb = yb_ref[...]
    nchunk = D_MODEL // MXU_DIM

    def pre(n):
        cols = slice(n * MXU_DIM, (n + 1) * MXU_DIM)
        gcols = slice(D_MODEL + n * MXU_DIM, D_MODEL + (n + 1) * MXU_DIM)
        return (_dot(hb, wg_ref[:, cols]), _dot(ya, wba_ref[:, cols]),
                _dot(hb, wg_ref[:, gcols]), _dot(yb, wbb_ref[:, cols]))

    ahead = [pre(n) for n in range(2)]
    mix = jnp.zeros(h.shape, F32)
    for n in range(nchunk):
        za, pa, zb, pb = ahead.pop(0)
        if n + 2 < nchunk:
            ahead.append(pre(n + 2))
        merged = jax.nn.sigmoid(za) * pa + jax.nn.sigmoid(zb) * pb
        mix = mix + _dot(merged.astype(BF16), wmix_ref[n * MXU_DIM:(n + 1) * MXU_DIM, :])
    o_ref[...] = _layer_norm(ALPHA * h + mix, g_ref[...], b_ref[...])


def _merge(h2d, ya, yb, wg, wba, wbb, wmix, g, b, tm):
    t = h2d.shape[0]
    row = lambda i: (i, 0)
    const = lambda i: (0, 0)
    return pl.pallas_call(
        _merge_kernel,
        grid=(t // tm,),
        in_specs=[pl.BlockSpec((tm, D_MODEL), row), pl.BlockSpec((tm, A_OUT), row),
                  pl.BlockSpec((tm, B_Q), row), pl.BlockSpec((D_MODEL, 2 * D_MODEL), const),
                  pl.BlockSpec((A_OUT, D_MODEL), const), pl.BlockSpec((B_Q, D_MODEL), const),
                  pl.BlockSpec((D_MODEL, D_MODEL), const), pl.BlockSpec((1, D_MODEL), const),
                  pl.BlockSpec((1, D_MODEL), const)],
        out_specs=pl.BlockSpec((tm, D_MODEL), row),
        out_shape=jax.ShapeDtypeStruct((t, D_MODEL), F32),
        compiler_params=_params("parallel"),
        name="merge",
    )(h2d, ya, yb, wg, wba, wbb, wmix, g, b)


def _xkv_kernel(mem_ref, w_ref, o_ref):
    o_ref[...] = _dot(mem_ref[...].astype(BF16), w_ref[...]).astype(o_ref.dtype)


def _xkv(mem, w):
    b, mlen, _ = mem.shape
    return pl.pallas_call(
        _xkv_kernel,
        grid=(b,),
        in_specs=[pl.BlockSpec((None, mlen, D_MODEL), lambda i: (i, 0, 0)),
                  pl.BlockSpec((D_MODEL, 2 * X_DIM), lambda i: (0, 0))],
        out_specs=pl.BlockSpec((None, mlen, 2 * X_DIM), lambda i: (i, 0, 0)),
        out_shape=jax.ShapeDtypeStruct((b, mlen, 2 * X_DIM), BF16),
        compiler_params=_params("parallel"),
        name="xkv",
    )(mem, w)


X_ROW_PARTS = 2


def _xattn_kernel(h_ref, kv_ref, wq_ref, wo_ref, g_ref, b_ref, o_ref):
    tm = h_ref.shape[0]
    k = kv_ref[:, :X_DIM]
    v = kv_ref[:, X_DIM:]
    part = tm // X_ROW_PARTS
    lane = lax.broadcasted_iota(jnp.int32, (part, X_DIM), 1)
    for r in range(X_ROW_PARTS):
        rows = slice(r * part, (r + 1) * part)
        h = h_ref[rows, :]
        q = _dot(h.astype(BF16), wq_ref[...]) * (SCALE * LOG2E)
        q4 = jnp.concatenate([jnp.where((lane // HEAD_DIM) == hd, q, 0.0) for hd in range(X_HEADS)],
                             axis=0).astype(BF16)
        s = _dot_nt(q4, k)
        e = jnp.exp2(s - jnp.max(s, axis=-1, keepdims=True))
        pv = _dot(e.astype(BF16), v) / jnp.sum(e, axis=-1, keepdims=True)
        o = pv[:part]
        for hd in range(1, X_HEADS):
            o = jnp.where((lane // HEAD_DIM) == hd, pv[hd * part:(hd + 1) * part], o)
        att = _dot(o.astype(BF16), wo_ref[...])
        o_ref[rows, :] = _layer_norm(ALPHA * h + att, g_ref[...], b_ref[...])


def _xattn(h2d, kv, wq, wo, g, b, seq, tm):
    t = h2d.shape[0]
    mlen = kv.shape[1]
    nseq = seq // tm
    row = lambda i: (i, 0)
    const = lambda i: (0, 0)
    return pl.pallas_call(
        _xattn_kernel,
        grid=(t // tm,),
        in_specs=[pl.BlockSpec((tm, D_MODEL), row),
                  pl.BlockSpec((None, mlen, 2 * X_DIM), lambda i: (i // nseq, 0, 0)),
                  pl.BlockSpec((D_MODEL, X_DIM), const), pl.BlockSpec((X_DIM, D_MODEL), const),
                  pl.BlockSpec((1, D_MODEL), const), pl.BlockSpec((1, D_MODEL), const)],
        out_specs=pl.BlockSpec((tm, D_MODEL), row),
        out_shape=jax.ShapeDtypeStruct((t, D_MODEL), F32),
        compiler_params=_params("parallel"),
        name="xattn",
    )(h2d, kv, wq, wo, g, b)


FFN_CHUNK = 256
FFN_AHEAD = 2
FFN_DOWN_GROUP = 6
FFN_ROW_PARTS = 1


def _ffn_kernel(h_ref, wup_ref, cw_ref, cb_ref, wdn_ref, g_ref, b_ref, o_ref, tail_sc, *, tiles_per_seq):
    tm = h_ref.shape[0]
    i = pl.program_id(0)

    @pl.when(i % tiles_per_seq == 0)
    def _():
        tail_sc[...] = jnp.zeros(tail_sc.shape, F32)

    nchunk = D_FF // FFN_CHUNK
    part = tm // FFN_ROW_PARTS
    top = lax.broadcasted_iota(jnp.int32, (SUBLANES, FFN_CHUNK), 0)

    def delayed(u, tail, d):
        ur = pltpu.roll(u, d, 0)
        first = jnp.where(top < d, pltpu.roll(tail, d, 0), ur[:SUBLANES])
        return jnp.concatenate([first, ur[SUBLANES:]], axis=0)

    for r in range(FFN_ROW_PARTS):
        rows = slice(r * part, (r + 1) * part)
        h = h_ref[rows, :]
        hb = h.astype(BF16)
        acc = jnp.zeros((part, D_MODEL), F32)

        def up(c, hb=hb):
            return [_dot(hb, wup_ref[:, base:base + FFN_CHUNK]) for base in (c * FFN_CHUNK, D_FF + c * FFN_CHUNK)]

        ahead = [up(c) for c in range(FFN_AHEAD)]
        acts = []
        for c in range(nchunk):
            u_cur = ahead.pop(0)
            if c + FFN_AHEAD < nchunk:
                ahead.append(up(c + FFN_AHEAD))
            halves = []
            for u, base in zip(u_cur, (c * FFN_CHUNK, D_FF + c * FFN_CHUNK)):
                cols = slice(base, base + FFN_CHUNK)
                tail = tail_sc[:, cols]
                tail_sc[:, cols] = u[part - SUBLANES:]
                halves.append(cw_ref[2:3, cols] * u + cw_ref[1:2, cols] * delayed(u, tail, 1)
                              + cw_ref[0:1, cols] * delayed(u, tail, 2) + cb_ref[:, cols])
            acts.append((_gelu_tanh(halves[0]) * halves[1]).astype(BF16))
            if len(acts) == FFN_DOWN_GROUP or c == nchunk - 1:
                k0 = (c + 1 - len(acts)) * FFN_CHUNK
                acc = acc + _dot(jnp.concatenate(acts, axis=1), wdn_ref[k0:(c + 1) * FFN_CHUNK, :])
                acts = []
        o_ref[rows, :] = _layer_norm(ALPHA * h + acc, g_ref[...], b_ref[...])


def _ffn(h2d, wup, cw, cb, wdn, g, b, seq, tm):
    t = h2d.shape[0]
    row = lambda i: (i, 0)
    const = lambda i: (0, 0)
    return pl.pallas_call(
        functools.partial(_ffn_kernel, tiles_per_seq=seq // tm),
        grid=(t // tm,),
        in_specs=[pl.BlockSpec((tm, D_MODEL), row),
                  pl.BlockSpec((D_MODEL, 2 * D_FF), const, pipeline_mode=pl.Buffered(1)),
                  pl.BlockSpec((CONV_W, 2 * D_FF), const), pl.BlockSpec((1, 2 * D_FF), const),
                  pl.BlockSpec((D_FF, D_MODEL), const, pipeline_mode=pl.Buffered(1)),
                  pl.BlockSpec((1, D_MODEL), const), pl.BlockSpec((1, D_MODEL), const)],
        out_specs=pl.BlockSpec((tm, D_MODEL), row),
        out_shape=jax.ShapeDtypeStruct((t, D_MODEL), F32),
        scratch_shapes=[pltpu.VMEM((SUBLANES, 2 * D_FF), F32)],
        compiler_params=_params("arbitrary"),
        name="ffn",
    )(h2d, wup, cw, cb, wdn, g, b)


def _rope_tables(seq):
    inv = ROPE_THETA ** (-jnp.arange(HALF, dtype=F32) / HALF)
    ang = jnp.arange(seq, dtype=F32)[:, None] * inv[None, :]
    cos, sin = jnp.cos(ang), jnp.sin(ang)
    zero, one = jnp.zeros_like(cos), jnp.ones_like(cos)
    cq = jnp.concatenate([cos, cos, cos, cos], axis=1)
    s1q = jnp.concatenate([-sin, zero, -sin, zero], axis=1)
    s2q = jnp.concatenate([zero, sin, zero, sin], axis=1)
    ck = jnp.concatenate([cos, cos, one, one], axis=1)
    s1k = jnp.concatenate([-sin, zero, zero, zero], axis=1)
    s2k = jnp.concatenate([zero, sin, zero, zero], axis=1)
    pos_c = (CMP_STRIDE * jnp.arange(seq // CMP_STRIDE) + CMP_LEN - 1).astype(F32)
    ang_c = pos_c[:, None] * inv[None, :]
    return (cq, s1q, s2q, ck, s1k, s2k), jnp.cos(ang_c), jnp.sin(ang_c)


def _overlap_t(seq):
    ncmp = seq // CMP_STRIDE
    nsel = seq // SLC_BLOCK
    c_start = CMP_STRIDE * jnp.arange(ncmp)
    s_start = SLC_BLOCK * jnp.arange(nsel)
    ov = jnp.clip(jnp.minimum(c_start[None, :] + CMP_LEN, s_start[:, None] + SLC_BLOCK)
                  - jnp.maximum(c_start[None, :], s_start[:, None]), 0).astype(F32) / CMP_LEN
    return ov.astype(BF16)


def _pack_w_in(w):
    o1 = A_QKV
    o2 = o1 + B_Q
    o3 = o2 + B_KV
    o4 = o3 + B_GATE
    wa, wq, wkv, wbg, wgate = w[:, :o1], w[:, o1:o2], w[:, o2:o3], w[:, o3:o4], w[:, o4:]
    kv = wkv.reshape(D_MODEL, 3, 2, B_KV_HEADS, HEAD_DIM)
    per_head = [jnp.concatenate([kv[:, 1, 0, hh], kv[:, 1, 1, hh], kv[:, 2, 0, hh], kv[:, 2, 1, hh]], axis=1)
                for hh in range(B_KV_HEADS)]
    cmp_blk = wkv[:, :2 * B_KV_HEADS * HEAD_DIM]
    per_gate = B_GROUP * 3
    gates = [jnp.pad(wbg[:, hh * per_gate:(hh + 1) * per_gate], ((0, 0), (0, LANES - per_gate)))
             for hh in range(B_KV_HEADS)]
    wp = jnp.concatenate([wa, wq] + per_head + [cmp_blk] + gates, axis=1)
    return wp.astype(BF16), wgate.astype(BF16)


def kernel(x, mem, w_in, cmp_pos, cmp_w1, cmp_b1, cmp_w2, w_branch_a, w_branch_b, w_mix_out, ln1_g, ln1_b,
           w_xq, w_xkv, w_xo, ln2_g, ln2_b, w_up, conv_w, conv_b, w_down, ln3_g, ln3_b):
    b, seq, _ = x.shape
    t = b * seq
    tabs, cos_c, sin_c = _rope_tables(seq)
    ovt = _overlap_t(seq)
    h = x.reshape(t, D_MODEL)
    for l in range(DEPTH):
        wp, wgate = _pack_w_in(w_in[l])
        za, zq, zkv, zc, zg = _project(h, wp, tabs, seq, tm=1024)
        kvc = _compress(zc.reshape(b, seq, MXU_DIM),
                        cmp_w1[l].reshape(2, CMP_LEN, HEAD_DIM, CMP_HIDDEN).astype(BF16),
                        cmp_pos[l], cmp_b1[l], cmp_w2[l].astype(BF16), cos_c, sin_c)
        ya = _attn_a(za.reshape(b, seq, A_QKV))
        yb = _nsa(zq.reshape(b, seq, B_Q), zkv.reshape(b, seq, 2 * MXU_DIM), kvc,
                  zg.reshape(b, seq, MXU_DIM), ovt)
        h = _merge(h, ya.reshape(t, A_OUT), yb.reshape(t, B_Q), wgate, w_branch_a[l].astype(BF16),
                   w_branch_b[l].astype(BF16), w_mix_out[l].astype(BF16),
                   ln1_g[l][None, :], ln1_b[l][None, :], tm=1024)
        xkv = _xkv(mem, w_xkv[l].astype(BF16))
        h = _xattn(h, xkv, w_xq[l].astype(BF16), w_xo[l].astype(BF16),
                   ln2_g[l][None, :], ln2_b[l][None, :], seq, tm=512)
        h = _ffn(h, w_up[l].astype(BF16), conv_w[l], conv_b[l][None, :], w_down[l].astype(BF16),
                 ln3_g[l][None, :], ln3_b[l][None, :], seq, tm=1024)
    return h.reshape(b, seq, D_MODEL)
```
